```python
import math
import jax, jax.numpy as jnp
from jax import lax
import numpy as np

D_MODEL = 1024
BATCH = 8
SEQ = 8192
DEPTH = 1

MEM_LEN = 256
SSD_HEADS = 16
SSD_HEAD_DIM = 64
SSD_INNER = SSD_HEADS * SSD_HEAD_DIM
SSD_GROUPS = 2
SSD_HEADS_PER_GROUP = SSD_HEADS // SSD_GROUPS
SSD_STATE = 128
SSD_CONV = 4
SSD_CONV_DIM = SSD_INNER + 2 * SSD_GROUPS * SSD_STATE
SSD_CHUNK = 128
RET_HEADS = 8
RET_QK_DIM = 64
RET_V_DIM = 128
RET_QK = RET_HEADS * RET_QK_DIM
RET_INNER = RET_HEADS * RET_V_DIM
RET_CHUNK = 128
ROPE_BASE = 10000.0
MIX_WIDTH = SSD_INNER + RET_INNER
IN_SIZES = (SSD_INNER, SSD_CONV_DIM, SSD_HEADS, RET_QK, RET_QK, RET_INNER, RET_INNER)
IN_COLS = SSD_INNER + SSD_CONV_DIM + SSD_HEADS + 2 * RET_QK + 2 * RET_INNER
XATTN_HEADS = 4
XATTN_HEAD_DIM = D_MODEL // XATTN_HEADS
D_FF = 2752
FFN_CONV = 3
ALPHA = (2.0 * DEPTH) ** 0.25
BETA = (8.0 * DEPTH) ** -0.25
EPS = 1e-5

kernel_name = "hybrid_ssd_retention_deepnorm_layer"


def _layer_norm(x, g, b):
    xf = x.astype(jnp.float32)
    mu = jnp.mean(xf, axis=-1, keepdims=True)
    var = jnp.mean(jnp.square(xf - mu), axis=-1, keepdims=True)
    return ((xf - mu) * lax.rsqrt(var + EPS) * g.astype(jnp.float32) + b.astype(jnp.float32)).astype(x.dtype)


def _split_cols(u, sizes):
    offs, acc = [], 0
    for s in sizes[:-1]:
        acc += s
        offs.append(acc)
    return jnp.split(u, offs, axis=-1)


def _causal_dwconv(u, w, b):
    k, ch = w.shape
    y = lax.conv_general_dilated(
        u, w[:, None, :].astype(u.dtype), window_strides=(1,), padding=[(k - 1, 0)],
        dimension_numbers=("NWC", "WIO", "NWC"), feature_group_count=ch)
    return y + b.astype(u.dtype)


def _ssd_chunked(xh, dt, a_head, b_in, c_in):
    bsz, seqlen, ng, nr, hp = xh.shape
    ns = b_in.shape[-1]
    q = SSD_CHUNK
    nc = seqlen // q
    f32 = jnp.float32
    xdt = (xh.astype(f32) * dt[..., None]).reshape(bsz, nc, q, ng, nr, hp)
    a = jnp.moveaxis((dt * a_head).reshape(bsz, nc, q, ng, nr), 2, -1)
    a_cs = jnp.cumsum(a, axis=-1)
    bc = b_in.astype(f32).reshape(bsz, nc, q, ng, ns)
    cc = c_in.astype(f32).reshape(bsz, nc, q, ng, ns)
    causal = jnp.tril(jnp.ones((q, q), dtype=bool))
    seg = jnp.exp(jnp.where(causal, a_cs[..., :, None] - a_cs[..., None, :], -jnp.inf))
    cb = jnp.einsum("bclgn,bcsgn->bcgls", cc, bc)
    att = cb[:, :, :, None] * seg
    y_diag = jnp.einsum("bcgrls,bcsgrp->bclgrp", att, xdt)
    decay_to_end = jnp.moveaxis(jnp.exp(a_cs[..., -1:] - a_cs), -1, 2)
    states = jnp.einsum("bcsgn,bcsgrp->bcgrpn", bc, xdt * decay_to_end[..., None])
    chunk_decay = jnp.exp(a_cs[..., -1])

    def step(h, inp):
        st, dec = inp
        return h * dec[..., None, None] + st, h

    h0 = jnp.zeros((bsz, ng, nr, hp, ns), f32)
    _, h_prev = lax.scan(step, h0, (jnp.moveaxis(states, 1, 0), jnp.moveaxis(chunk_decay, 1, 0)))
    h_prev = jnp.moveaxis(h_prev, 0, 1)
    decay_from_start = jnp.moveaxis(jnp.exp(a_cs), -1, 2)
    y_off = jnp.einsum("bclgn,bcgrpn->bclgrp", cc, h_prev) * decay_from_start[..., None]
    return (y_diag + y_off).reshape(bsz, seqlen, ng, nr, hp)


def _rotary(u, cos, sin):
    u = u.astype(jnp.float32)
    u1, u2 = jnp.split(u, 2, axis=-1)
    c = cos[None, :, None, :]
    s = sin[None, :, None, :]
    return jnp.concatenate([u1 * c - u2 * s, u1 * s + u2 * c], axis=-1)


def _retention_chunkwise(q, k, v, log_gamma):
    bsz, seqlen, nh, dk = q.shape
    dv = v.shape[-1]
    n = RET_CHUNK
    nc = seqlen // n
    f32 = jnp.float32
    qc = q.reshape(bsz, nc, n, nh, dk)
    kc = k.reshape(bsz, nc, n, nh, dk)
    vc = v.astype(f32).reshape(bsz, nc, n, nh, dv)
    pos = jnp.arange(n, dtype=f32)
    dist = pos[:, None] - pos[None, :]
    decay = jnp.exp(jnp.where(dist >= 0, dist * log_gamma[:, None, None], -jnp.inf))
    s = jnp.einsum("bcnhd,bcmhd->bchnm", qc, kc) * decay
    y_in = jnp.einsum("bchnm,bcmhe->bcnhe", s, vc)
    k_dec = jnp.exp((n - 1 - pos)[:, None] * log_gamma)
    states = jnp.einsum("bcmhd,bcmhe->bchde", kc * k_dec[:, :, None], vc)
    chunk_decay = jnp.exp(n * log_gamma)

    def step(r, st):
        return r * chunk_decay[:, None, None] + st, r

    r0 = jnp.zeros((bsz, nh, dk, dv), f32)
    _, r_prev = lax.scan(step, r0, jnp.moveaxis(states, 1, 0))
    r_prev = jnp.moveaxis(r_prev, 0, 1)
    q_dec = jnp.exp((pos + 1.0)[:, None] * log_gamma)
    y_cross = jnp.einsum("bcnhd,bchde->bcnhe", qc * q_dec[:, :, None], r_prev)
    return (y_in + y_cross).reshape(bsz, seqlen, nh, dv)


def _mixer(x, w_in, conv_w, conv_b, dt_bias, a_log, d_skip, norm_w, gn_w, gn_b, w_out, cos, sin):
    f32 = jnp.float32
    bsz, seqlen, _ = x.shape
    proj = x @ w_in
    z, xbc, dt_raw, q, k, v, g = _split_cols(proj, IN_SIZES)
    xbc = jax.nn.silu(_causal_dwconv(xbc, conv_w, conv_b))
    xs, b_in, c_in = _split_cols(xbc, (SSD_INNER, SSD_GROUPS * SSD_STATE, SSD_GROUPS * SSD_STATE))
    dt = jax.nn.softplus(dt_raw.astype(f32) + dt_bias.astype(f32))
    a_head = -jnp.exp(a_log.astype(f32))
    xh = xs.reshape(bsz, seqlen, SSD_GROUPS, SSD_HEADS_PER_GROUP, SSD_HEAD_DIM)
    y = _ssd_chunked(
        xh, dt.reshape(bsz, seqlen, SSD_GROUPS, SSD_HEADS_PER_GROUP),
        a_head.reshape(SSD_GROUPS, SSD_HEADS_PER_GROUP),
        b_in.reshape(bsz, seqlen, SSD_GROUPS, SSD_STATE),
        c_in.reshape(bsz, seqlen, SSD_GROUPS, SSD_STATE))
    y = y + d_skip.astype(f32).reshape(SSD_GROUPS, SSD_HEADS_PER_GROUP)[:, :, None] * xh.astype(f32)
    u = (y.reshape(bsz, seqlen, SSD_INNER) * jax.nn.silu(z.astype(f32))).reshape(bsz, seqlen, SSD_GROUPS, -1)
    u = u * lax.rsqrt(jnp.mean(jnp.square(u), axis=-1, keepdims=True) + EPS)
    y_ssd = u.reshape(bsz, seqlen, SSD_INNER) * norm_w.astype(f32)
    log_gamma = jnp.log1p(-jnp.exp2(-5.0 - jnp.arange(RET_HEADS, dtype=f32)))
    qh = _rotary(q.reshape(bsz, seqlen, RET_HEADS, RET_QK_DIM), cos, sin)
    kh = _rotary(k.reshape(bsz, seqlen, RET_HEADS, RET_QK_DIM), cos, sin) * (RET_QK_DIM ** -0.5)
    yr = _retention_chunkwise(qh, kh, v.reshape(bsz, seqlen, RET_HEADS, RET_V_DIM), log_gamma)
    mu = jnp.mean(yr, axis=-1, keepdims=True)
    var = jnp.mean(jnp.square(yr - mu), axis=-1, keepdims=True)
    yr = ((yr - mu) * lax.rsqrt(var + EPS)).reshape(bsz, seqlen, RET_INNER)
    yr = yr * gn_w.astype(f32) + gn_b.astype(f32)
    y_ret = jax.nn.silu(g.astype(f32)) * yr
    y_mix = jnp.concatenate([y_ssd, y_ret], axis=-1).astype(x.dtype)
    return y_mix @ w_out


def _memory_cross_attention(x, mem, w_q, w_k, w_v, w_o):
    bsz, seqlen, _ = x.shape
    m = mem.shape[1]
    qx = (x @ w_q).reshape(bsz, seqlen, XATTN_HEADS, XATTN_HEAD_DIM)
    km = (mem @ w_k).reshape(bsz, m, XATTN_HEADS, XATTN_HEAD_DIM)
    vm = (mem @ w_v).reshape(bsz, m, XATTN_HEADS, XATTN_HEAD_DIM)
    s = jnp.einsum("blhd,bmhd->bhlm", qx, km).astype(jnp.float32) * (XATTN_HEAD_DIM ** -0.5)
    p = jax.nn.softmax(s, axis=-1).astype(vm.dtype)
    o = jnp.einsum("bhlm,bmhd->blhd", p, vm).reshape(bsz, seqlen, D_MODEL)
    return o @ w_o


def _conv_glu_ffn(x, w_up, b_up, conv_w, conv_b, w_down):
    h = x @ w_up + b_up
    h = _causal_dwconv(h, conv_w, conv_b)
    a, u = jnp.split(h, 2, axis=-1)
    return (jax.nn.silu(a) * u) @ w_down


def setup_inputs(seed: int = 0) -> dict:
    key = jax.random.key(seed)
    ks = jax.random.split(key, 32)
    f32 = jnp.float32

    def nrm(k, shape, scale):
        return jax.random.normal(k, shape, f32) * scale

    nl = DEPTH
    dt0 = jnp.exp(jax.random.uniform(ks[5], (nl, SSD_HEADS), f32, math.log(1e-3), math.log(1e-1)))
    return {
        "x": nrm(ks[0], (BATCH, SEQ, D_MODEL), 1.0),
        "mem": nrm(ks[1], (BATCH, MEM_LEN, D_MODEL), 1.0),
        "w_in": nrm(ks[2], (nl, D_MODEL, IN_COLS), D_MODEL ** -0.5),
        "ssd_conv_w": nrm(ks[3], (nl, SSD_CONV, SSD_CONV_DIM), SSD_CONV ** -0.5),
        "ssd_conv_b": nrm(ks[4], (nl, SSD_CONV_DIM), 0.02),
        "ssd_dt_bias": dt0 + jnp.log(-jnp.expm1(-dt0)),
        "ssd_a_log": jnp.log(jax.random.uniform(ks[6], (nl, SSD_HEADS), f32, 1.0, 16.0)),
        "ssd_d": 1.0 + nrm(ks[7], (nl, SSD_HEADS), 0.02),
        "ssd_norm_w": 1.0 + nrm(ks[8], (nl, SSD_INNER), 0.02),
        "ret_gn_w": 1.0 + nrm(ks[9], (nl, RET_INNER), 0.02),
        "ret_gn_b": nrm(ks[10], (nl, RET_INNER), 0.02),
        "w_mix_out": nrm(ks[11], (nl, MIX_WIDTH, D_MODEL), BETA * MIX_WIDTH ** -0.5),
        "ln1_g": 1.0 + nrm(ks[12], (nl, D_MODEL), 0.02),
        "ln1_b": nrm(ks[13], (nl, D_MODEL), 0.02),
        "w_xq": nrm(ks[14], (nl, D_MODEL, D_MODEL), D_MODEL ** -0.5),
        "w_xk": nrm(ks[15], (nl, D_MODEL, D_MODEL), D_MODEL ** -0.5),
        "w_xv": nrm(ks[16], (nl, D_MODEL, D_MODEL), D_MODEL ** -0.5),
        "w_xo": nrm(ks[17], (nl, D_MODEL, D_MODEL), BETA * D_MODEL ** -0.5),
        "ln2_g": 1.0 + nrm(ks[18], (nl, D_MODEL), 0.02),
        "ln2_b": nrm(ks[19], (nl, D_MODEL), 0.02),
        "w_ffn_up": nrm(ks[20], (nl, D_MODEL, 2 * D_FF), D_MODEL ** -0.5),
        "b_ffn_up": nrm(ks[21], (nl, 2 * D_FF), 0.02),
        "ffn_conv_w": nrm(ks[22], (nl, FFN_CONV, 2 * D_FF), FFN_CONV ** -0.5),
        "ffn_conv_b": nrm(ks[23], (nl, 2 * D_FF), 0.02),
        "w_ffn_down": nrm(ks[24], (nl, D_FF, D_MODEL), BETA * D_FF ** -0.5),
        "ln3_g": 1.0 + nrm(ks[25], (nl, D_MODEL), 0.02),
        "ln3_b": nrm(ks[26], (nl, D_MODEL), 0.02),
    }


def reference(x, mem, w_in, ssd_conv_w, ssd_conv_b, ssd_dt_bias, ssd_a_log, ssd_d, ssd_norm_w,
              ret_gn_w, ret_gn_b, w_mix_out, ln1_g, ln1_b, w_xq, w_xk, w_xv, w_xo, ln2_g, ln2_b,
              w_ffn_up, b_ffn_up, ffn_conv_w, ffn_conv_b, w_ffn_down, ln3_g, ln3_b):
    seqlen = x.shape[1]
    pos = jnp.arange(seqlen, dtype=jnp.float32)
    freqs = 1.0 / (ROPE_BASE ** jnp.linspace(0.0, 1.0, RET_QK_DIM // 2, dtype=jnp.float32))
    ang = pos[:, None] * freqs[None, :]
    cos, sin = jnp.cos(ang), jnp.sin(ang)
    for l in range(DEPTH):
        mix = _mixer(x, w_in[l], ssd_conv_w[l], ssd_conv_b[l], ssd_dt_bias[l], ssd_a_log[l], ssd_d[l],
                     ssd_norm_w[l], ret_gn_w[l], ret_gn_b[l], w_mix_out[l], cos, sin)
        x = _layer_norm(ALPHA * x + mix, ln1_g[l], ln1_b[l])
        xa = _memory_cross_attention(x, mem, w_xq[l], w_xk[l], w_xv[l], w_xo[l])
        x = _layer_norm(ALPHA * x + xa, ln2_g[l], ln2_b[l])
        ff = _conv_glu_ffn(x, w_ffn_up[l], b_ffn_up[l], ffn_conv_w[l], ffn_conv_b[l], w_ffn_down[l])
        x = _layer_norm(ALPHA * x + ff, ln3_g[l], ln3_b[l])
    return x
```

```python
import functools

import numpy as np
import jax
import jax.numpy as jnp
from jax import lax
from jax.experimental import pallas as pl
from jax.experimental.pallas import tpu as pltpu

F32 = jnp.float32
BF16 = jnp.bfloat16

D_MODEL = 1024
DEPTH = 1
SSD_HEADS = 16
SSD_HEAD_DIM = 64
SSD_INNER = SSD_HEADS * SSD_HEAD_DIM
SSD_GROUPS = 2
SSD_STATE = 128
SSD_CONV = 4
SSD_CONV_DIM = SSD_INNER + 2 * SSD_GROUPS * SSD_STATE
GROUP_WIDTH = SSD_INNER // SSD_GROUPS
RET_HEADS = 8
RET_QK_DIM = 64
RET_V_DIM = 128
RET_QK = RET_HEADS * RET_QK_DIM
RET_INNER = RET_HEADS * RET_V_DIM
CHUNK = 128
ROPE_BASE = 10000.0
MIX_WIDTH = SSD_INNER + RET_INNER
XATTN_HEADS = 4
XATTN_HEAD_DIM = D_MODEL // XATTN_HEADS
D_FF = 2752
FFN_CONV = 3
ALPHA = (2.0 * DEPTH) ** 0.25
EPS = 1e-5

LANES = 128
SUBLANES = 8
VMEM_LIMIT_BYTES = 56 * 1024 * 1024

Z0 = 0
XBC0 = Z0 + SSD_INNER
DT0 = XBC0 + SSD_CONV_DIM
Q0 = DT0 + LANES
K0 = Q0 + RET_QK
V0 = K0 + RET_QK
G0 = V0 + RET_INNER
IN_PACKED = G0 + RET_INNER

D_FF_PAD = ((D_FF + LANES - 1) // LANES) * LANES
FFN_BLOCK = 256
assert D_FF_PAD % FFN_BLOCK == 0


def _dot(a, b):
    return jnp.dot(a, b, preferred_element_type=F32)


def _dot_nt(a, b):
    return lax.dot_general(a, b, (((1,), (1,)), ((), ())), preferred_element_type=F32)


def _dot_tn(a, b):
    return lax.dot_general(a, b, (((0,), (0,)), ((), ())), preferred_element_type=F32)


def _silu(v):
    return v / (1.0 + jnp.exp(-v))


def _layer_norm(r, g, b):
    mu = jnp.mean(r, axis=-1, keepdims=True)
    rc = r - mu
    var = jnp.mean(rc * rc, axis=-1, keepdims=True)
    return rc * lax.rsqrt(var + EPS) * g + b


def _split2(w):
    hi = w.astype(BF16)
    lo = (w - hi.astype(F32)).astype(BF16)
    return hi, lo


def _split3(w):
    hi = w.astype(BF16)
    r1 = w - hi.astype(F32)
    mid = r1.astype(BF16)
    lo = (r1 - mid.astype(F32)).astype(BF16)
    return hi, mid, lo


def _ret_log_gamma():
    return np.log1p(-np.exp2(-5.0 - np.arange(RET_HEADS, dtype=np.float32))).astype(np.float32)


def _memkv_kernel(mem_ref, wk_ref, wv_ref, kt_ref, v_ref):
    m = mem_ref[0].astype(BF16)
    kt_ref[0] = lax.dot_general(wk_ref[...], m, (((0,), (1,)), ((), ())),
                                preferred_element_type=F32).astype(BF16)
    v_ref[0] = _dot(m, wv_ref[...]).astype(BF16)


def _memkv(mem, wk, wv):
    bsz, mlen, _ = mem.shape
    const = lambda b: (0, 0)
    return pl.pallas_call(
        _memkv_kernel,
        grid=(bsz,),
        in_specs=[
            pl.BlockSpec((1, mlen, D_MODEL), lambda b: (b, 0, 0)),
            pl.BlockSpec((D_MODEL, D_MODEL), const),
            pl.BlockSpec((D_MODEL, D_MODEL), const),
        ],
        out_specs=[
            pl.BlockSpec((1, D_MODEL, mlen), lambda b: (b, 0, 0)),
            pl.BlockSpec((1, mlen, D_MODEL), lambda b: (b, 0, 0)),
        ],
        out_shape=[
            jax.ShapeDtypeStruct((bsz, D_MODEL, mlen), BF16),
            jax.ShapeDtypeStruct((bsz, mlen, D_MODEL), BF16),
        ],
        compiler_params=pltpu.CompilerParams(
            dimension_semantics=("arbitrary",), vmem_limit_bytes=VMEM_LIMIT_BYTES),
        name="memkv",
    )(mem, wk, wv)


def _mixer_kernel(x_ref, cos_ref, sin_ref, win_ref, convw_ref, convb_ref, dtb_ref, alog_ref,
                  dskip_ref, normw_ref, gnw_ref, gnb_ref, wout_ref, lng_ref, lnb_ref,
                  out_ref,
                  z_s, xbc_s, xa_s, dt_s, q_s, k_s, v_s, g_s, ymix_s, hs_s, r_s,
                  dec_s, qd_s, kd_s, exp_s, tri_s, *, tile):
    b = pl.program_id(0)
    t = pl.program_id(1)
    log_gamma = _ret_log_gamma()

    @pl.when(jnp.logical_and(b == 0, t == 0))
    def _init_constants():
        row = lax.broadcasted_iota(jnp.int32, (CHUNK, CHUNK), 0)
        col = lax.broadcasted_iota(jnp.int32, (CHUNK, CHUNK), 1)
        tri_s[...] = jnp.where(row >= col, 1.0, 0.0).astype(BF16)
        erow = lax.broadcasted_iota(jnp.int32, (LANES, SSD_INNER), 0)
        ecol = lax.broadcasted_iota(jnp.int32, (LANES, SSD_INNER), 1)
        lo = erow * SSD_HEAD_DIM
        exp_s[...] = jnp.where(jnp.logical_and(ecol >= lo, ecol < lo + SSD_HEAD_DIM), 1.0, 0.0).astype(BF16)
        dist = (row - col).astype(F32)
        pos = row.astype(F32)
        half = RET_QK_DIM // 2
        for h in range(RET_HEADS):
            lg = float(log_gamma[h])
            in_head = jnp.bitwise_and(jnp.right_shift(col, 5), 1) == (h % 2)
            headmask = jnp.where(in_head, 1.0, 0.0)
            dec_s[h] = (RET_QK_DIM ** -0.5) * jnp.exp(jnp.where(dist >= 0, dist * lg, -jnp.inf))
            qd_s[h] = jnp.exp((pos + 1.0) * lg) * headmask
            kd_s[h] = (RET_QK_DIM ** -0.5) * jnp.exp((CHUNK - 1.0 - pos) * lg) * headmask

    @pl.when(t == 0)
    def _reset_state():
        hs_s[...] = jnp.zeros_like(hs_s)
        r_s[...] = jnp.zeros_like(r_s)
        xbc_s[0:SUBLANES, :] = jnp.zeros((SUBLANES, SSD_CONV_DIM), F32)

    x = x_ref[0]
    xb = x.astype(BF16)

    def proj(c0, n):
        return _dot(xb, win_ref[:, c0:c0 + n])

    z_s[...] = _silu(proj(Z0, SSD_INNER))
    xbc_s[SUBLANES:SUBLANES + tile, :] = proj(XBC0, SSD_CONV_DIM)
    dtr = proj(DT0, LANES) + dtb_ref[...]
    dt_s[...] = jnp.maximum(dtr, 0.0) + jnp.log1p(jnp.exp(-jnp.abs(dtr)))
    cos = cos_ref[...]
    sin = sin_ref[...]
    qraw = proj(Q0, RET_QK)
    kraw = proj(K0, RET_QK)
    for j in range(RET_QK // LANES):
        sl = slice(j * LANES, (j + 1) * LANES)
        qj = qraw[:, sl]
        kj = kraw[:, sl]
        q_s[:, sl] = qj * cos + pltpu.roll(qj, LANES // 2, 1) * sin
        k_s[:, sl] = kj * cos + pltpu.roll(kj, LANES // 2, 1) * sin
    v_s[...] = proj(V0, RET_INNER).astype(BF16)
    g_s[...] = _silu(proj(G0, RET_INNER))

    conv = convb_ref[...]
    for k in range(SSD_CONV):
        off = SUBLANES - (SSD_CONV - 1) + k
        conv = conv + convw_ref[k:k + 1, :] * xbc_s[off:off + tile, :]
    xa_s[...] = _silu(conv)
    xbc_s[0:SUBLANES, :] = xbc_s[tile:tile + SUBLANES, :]

    lane = lax.broadcasted_iota(jnp.int32, (1, LANES), 1)
    a_head = jnp.where(lane < SSD_HEADS, -jnp.exp(alog_ref[...]), 0.0)
    row = lax.broadcasted_iota(jnp.int32, (CHUNK, CHUNK), 0)
    col = lax.broadcasted_iota(jnp.int32, (CHUNK, CHUNK), 1)
    causal = row >= col
    first_half = col < SSD_HEAD_DIM
    tri = tri_s[...]
    pair_masks = []
    for j in range(2):
        pair_masks.append(jnp.where(jnp.bitwise_and(jnp.right_shift(col, 5), 1) == j, 1.0, 0.0))

    def expand(w):
        hi, lo = _split2(w)
        return _dot(hi, exp_s[...]) + _dot(lo, exp_s[...])

    def chunk_body(c, carry):
        r0 = pl.multiple_of(c * CHUNK, CHUNK)
        rows = pl.ds(r0, CHUNK)

        dt = dt_s[rows, :]
        a = dt * a_head
        a_hi, a_mid, a_lo = _split3(a)
        acs = _dot(tri, a_hi) + _dot(tri, a_mid) + _dot(tri, a_lo)
        tot = acs[CHUNK - 1:CHUNK, :]
        w1x = expand(dt * jnp.exp(tot - acs))
        dfsx = expand(jnp.exp(acs))
        cdx = expand(jnp.broadcast_to(jnp.exp(tot), (SUBLANES, LANES)))[0:1, :]
        acs_t = acs.T
        dt_t = dt.T
        for g in range(SSD_GROUPS):
            gs = slice(g * GROUP_WIDTH, (g + 1) * GROUP_WIDTH)
            b0 = SSD_INNER + g * SSD_STATE
            c0 = SSD_INNER + SSD_GROUPS * SSD_STATE + g * SSD_STATE
            bm = xa_s[rows, b0:b0 + SSD_STATE].astype(BF16)
            cm = xa_s[rows, c0:c0 + SSD_STATE].astype(BF16)
            xg = xa_s[rows, gs]
            cb = _dot_nt(cm, bm)
            h_prev = hs_s[g]
            y_off = _dot(cm, h_prev.astype(BF16)) * dfsx[:, gs]
            states = _dot_tn(bm, (xg * w1x[:, gs]).astype(BF16))
            hs_s[g] = h_prev * cdx[:, gs] + states
            us = []
            for pr in range(GROUP_WIDTH // LANES):
                att = []
                for j in range(2):
                    h = g * (SSD_HEADS // SSD_GROUPS) + 2 * pr + j
                    seg = jnp.exp(jnp.where(causal, acs[:, h:h + 1] - acs_t[h:h + 1, :], -jnp.inf))
                    att.append((cb * seg * dt_t[h:h + 1, :]).astype(BF16))
                ps = slice(pr * LANES, (pr + 1) * LANES)
                xp = xg[:, ps]
                rhs = jnp.concatenate([jnp.where(first_half, xp, 0.0).astype(BF16),
                                       jnp.where(first_half, 0.0, xp).astype(BF16)], axis=0)
                y = _dot(jnp.concatenate(att, axis=1), rhs)
                gl = slice(g * GROUP_WIDTH + pr * LANES, g * GROUP_WIDTH + (pr + 1) * LANES)
                y = y + y_off[:, ps] + dskip_ref[:, gl] * xp
                us.append(y * z_s[rows, gl])
            ssq = us[0] * us[0]
            for u in us[1:]:
                ssq = ssq + u * u
            scale = lax.rsqrt(jnp.sum(ssq, axis=-1, keepdims=True) * (1.0 / GROUP_WIDTH) + EPS)
            for pr, u in enumerate(us):
                gl = slice(g * GROUP_WIDTH + pr * LANES, g * GROUP_WIDTH + (pr + 1) * LANES)
                ymix_s[rows, gl] = (u * scale * normw_ref[:, gl]).astype(BF16)

        for pr in range(RET_HEADS // 2):
            ps = slice(pr * LANES, (pr + 1) * LANES)
            qp = q_s[rows, ps]
            kp = k_s[rows, ps]
            kpb = kp.astype(BF16)
            for j in range(2):
                h = 2 * pr + j
                hl = slice(h * RET_V_DIM, (h + 1) * RET_V_DIM)
                s = _dot_nt((qp * pair_masks[j]).astype(BF16), kpb) * dec_s[h]
                qd = (qp * qd_s[h]).astype(BF16)
                vh = v_s[rows, hl]
                r_prev = r_s[h]
                y = _dot(jnp.concatenate([s.astype(BF16), qd], axis=1),
                         jnp.concatenate([vh, r_prev.astype(BF16)], axis=0))
                r_s[h] = r_prev * float(np.exp(np.float32(CHUNK) * log_gamma[h])) + _dot_tn(
                    (kp * kd_s[h]).astype(BF16), vh)
                mu = jnp.mean(y, axis=-1, keepdims=True)
                yc = y - mu
                var = jnp.mean(yc * yc, axis=-1, keepdims=True)
                yn = yc * lax.rsqrt(var + EPS) * gnw_ref[:, hl] + gnb_ref[:, hl]
                ymix_s[rows, SSD_INNER + h * RET_V_DIM:SSD_INNER + (h + 1) * RET_V_DIM] = (
                    g_s[rows, hl] * yn).astype(BF16)
        return carry

    lax.fori_loop(0, tile // CHUNK, chunk_body, 0)

    mix = _dot(ymix_s[...], wout_ref[...])
    out_ref[0] = _layer_norm(ALPHA * x + mix, lng_ref[...], lnb_ref[...])


def _mixer(x, cos_t, sin_t, w_in_p, conv_w, conv_b, dtb, alog, dskip, normw, gnw, gnb, w_out, ln_g, ln_b, tile):
    bsz, seqlen, _ = x.shape
    nt = seqlen // tile
    const = lambda b, t: (0, 0)

    def cspec(shape):
        return pl.BlockSpec(shape, const, pipeline_mode=pl.Buffered(1))

    return pl.pallas_call(
        functools.partial(_mixer_kernel, tile=tile),
        grid=(bsz, nt),
        in_specs=[
            pl.BlockSpec((1, tile, D_MODEL), lambda b, t: (b, t, 0)),
            pl.BlockSpec((tile, LANES), lambda b, t: (t, 0)),
            pl.BlockSpec((tile, LANES), lambda b, t: (t, 0)),
            cspec((D_MODEL, IN_PACKED)),
            cspec((SSD_CONV, SSD_CONV_DIM)),
            cspec((1, SSD_CONV_DIM)),
            cspec((1, LANES)),
            cspec((1, LANES)),
            cspec((1, SSD_INNER)),
            cspec((1, SSD_INNER)),
            cspec((1, RET_INNER)),
            cspec((1, RET_INNER)),
            cspec((MIX_WIDTH, D_MODEL)),
            cspec((1, D_MODEL)),
            cspec((1, D_MODEL)),
        ],
        out_specs=pl.BlockSpec((1, tile, D_MODEL), lambda b, t: (b, t, 0)),
        out_shape=jax.ShapeDtypeStruct((bsz, seqlen, D_MODEL), F32),
        scratch_shapes=[
            pltpu.VMEM((tile, SSD_INNER), F32),
            pltpu.VMEM((tile + SUBLANES, SSD_CONV_DIM), F32),
            pltpu.VMEM((tile, SSD_CONV_DIM), F32),
            pltpu.VMEM((tile, LANES), F32),
            pltpu.VMEM((tile, RET_QK), F32),
            pltpu.VMEM((tile, RET_QK), F32),
            pltpu.VMEM((tile, RET_INNER), BF16),
            pltpu.VMEM((tile, RET_INNER), F32),
            pltpu.VMEM((tile, MIX_WIDTH), BF16),
            pltpu.VMEM((SSD_GROUPS, SSD_STATE, GROUP_WIDTH), F32),
            pltpu.VMEM((RET_HEADS, LANES, RET_V_DIM), F32),
            pltpu.VMEM((RET_HEADS, CHUNK, CHUNK), F32),
            pltpu.VMEM((RET_HEADS, CHUNK, LANES), F32),
            pltpu.VMEM((RET_HEADS, CHUNK, LANES), F32),
            pltpu.VMEM((LANES, SSD_INNER), BF16),
            pltpu.VMEM((CHUNK, CHUNK), BF16),
        ],
        compiler_params=pltpu.CompilerParams(
            dimension_semantics=("arbitrary", "arbitrary"), vmem_limit_bytes=VMEM_LIMIT_BYTES),
        name="mixer",
    )(x, cos_t, sin_t, w_in_p, conv_w, conv_b, dtb, alog, dskip, normw, gnw, gnb, w_out, ln_g, ln_b)


def _xattn_kernel(x_ref, kt_ref, v_ref, wq_ref, wo_ref, lng_ref, lnb_ref, out_ref, o_s):
    x = x_ref[0]
    q = _dot(x.astype(BF16), wq_ref[...]).astype(BF16)
    for h in range(XATTN_HEADS):
        hs = slice(h * XATTN_HEAD_DIM, (h + 1) * XATTN_HEAD_DIM)
        s = _dot(q[:, hs], kt_ref[0, hs, :]) * (XATTN_HEAD_DIM ** -0.5)
        m = jnp.max(s, axis=-1, keepdims=True)
        p = jnp.exp(s - m)
        p = p / jnp.sum(p, axis=-1, keepdims=True)
        o_s[:, hs] = _dot(p.astype(BF16), v_ref[0, :, hs]).astype(BF16)
    xa = _dot(o_s[...], wo_ref[...])
    out_ref[0] = _layer_norm(ALPHA * x + xa, lng_ref[...], lnb_ref[...])


def _xattn(x, kt, vm, wq, wo, ln_g, ln_b, tile):
    bsz, seqlen, _ = x.shape
    mlen = vm.shape[1]
    const = lambda b, t: (0, 0)

    def cspec(shape):
        return pl.BlockSpec(shape, const, pipeline_mode=pl.Buffered(1))

    return pl.pallas_call(
        _xattn_kernel,
        grid=(bsz, seqlen // tile),
        in_specs=[
            pl.BlockSpec((1, tile, D_MODEL), lambda b, t: (b, t, 0)),
            pl.BlockSpec((1, D_MODEL, mlen), lambda b, t: (b, 0, 0)),
            pl.BlockSpec((1, mlen, D_MODEL), lambda b, t: (b, 0, 0)),
            cspec((D_MODEL, D_MODEL)),
            cspec((D_MODEL, D_MODEL)),
            cspec((1, D_MODEL)),
            cspec((1, D_MODEL)),
        ],
        out_specs=pl.BlockSpec((1, tile, D_MODEL), lambda b, t: (b, t, 0)),
        out_shape=jax.ShapeDtypeStruct((bsz, seqlen, D_MODEL), F32),
        scratch_shapes=[pltpu.VMEM((tile, D_MODEL), BF16)],
        compiler_params=pltpu.CompilerParams(
            dimension_semantics=("arbitrary", "arbitrary"), vmem_limit_bytes=VMEM_LIMIT_BYTES),
        name="xattn",
    )(x, kt, vm, wq, wo, ln_g, ln_b)


def _ffn_kernel(x_ref, wup_ref, bup_ref, convw_ref, convb_ref, wdown_ref, lng_ref, lnb_ref,
                out_ref, ext_s, carry_s, acc_s, *, tile):
    t = pl.program_id(1)

    @pl.when(t == 0)
    def _reset_halo():
        carry_s[...] = jnp.zeros_like(carry_s)

    x = x_ref[0]
    xb = x.astype(BF16)

    def conv_half(c0):
        cs = slice(c0, c0 + FFN_BLOCK)
        h = _dot(xb, wup_ref[:, cs]) + bup_ref[:, cs]
        ext_s[0:SUBLANES, :] = carry_s[:, cs]
        ext_s[SUBLANES:SUBLANES + tile, :] = h
        carry_s[:, cs] = h[tile - SUBLANES:tile, :]
        y = convb_ref[:, cs]
        for k in range(FFN_CONV):
            off = SUBLANES - (FFN_CONV - 1) + k
            y = y + convw_ref[k:k + 1, cs] * ext_s[off:off + tile, :]
        return y

    for j in range(D_FF_PAD // FFN_BLOCK):
        a = conv_half(j * FFN_BLOCK)
        u = conv_half(D_FF_PAD + j * FFN_BLOCK)
        gated = (_silu(a) * u).astype(BF16)
        part = _dot(gated, wdown_ref[j * FFN_BLOCK:(j + 1) * FFN_BLOCK, :])
        if j == 0:
            acc_s[...] = part
        else:
            acc_s[...] += part
    out_ref[0] = _layer_norm(ALPHA * x + acc_s[...], lng_ref[...], lnb_ref[...])


def _ffn(x, w_up, b_up, conv_w, conv_b, w_down, ln_g, ln_b, tile):
    bsz, seqlen, _ = x.shape
    const = lambda b, t: (0, 0)

    def cspec(shape):
        return pl.BlockSpec(shape, const, pipeline_mode=pl.Buffered(1))

    return pl.pallas_call(
        functools.partial(_ffn_kernel, tile=tile),
        grid=(bsz, seqlen // tile),
        in_specs=[
            pl.BlockSpec((1, tile, D_MODEL), lambda b, t: (b, t, 0)),
            cspec((D_MODEL, 2 * D_FF_PAD)),
            cspec((1, 2 * D_FF_PAD)),
            cspec((FFN_CONV, 2 * D_FF_PAD)),
            cspec((1, 2 * D_FF_PAD)),
            cspec((D_FF_PAD, D_MODEL)),
            cspec((1, D_MODEL)),
            cspec((1, D_MODEL)),
        ],
        out_specs=pl.BlockSpec((1, tile, D_MODEL), lambda b, t: (b, t, 0)),
        out_shape=jax.ShapeDtypeStruct((bsz, seqlen, D_MODEL), F32),
        scratch_shapes=[
            pltpu.VMEM((tile + SUBLANES, FFN_BLOCK), F32),
            pltpu.VMEM((SUBLANES, 2 * D_FF_PAD), F32),
            pltpu.VMEM((tile, D_MODEL), F32),
        ],
        compiler_params=pltpu.CompilerParams(
            dimension_semantics=("arbitrary", "arbitrary"), vmem_limit_bytes=VMEM_LIMIT_BYTES),
        name="ffn",
    )(x, w_up, b_up, conv_w, conv_b, w_down, ln_g, ln_b)


def _qk_perm():
    half = RET_QK_DIM // 2
    idx = []
    for pr in range(RET_HEADS // 2):
        for part in range(2):
            for j in range(2):
                h = 2 * pr + j
                idx.extend(range(h * RET_QK_DIM + part * half, h * RET_QK_DIM + (part + 1) * half))
    return np.asarray(idx, dtype=np.int32)


def _pad_last(a, n):
    return jnp.pad(a, [(0, 0)] * (a.ndim - 1) + [(0, n - a.shape[-1])])


def _pack_in_proj(w_in):
    o = np.cumsum((0, SSD_INNER, SSD_CONV_DIM, SSD_HEADS, RET_QK, RET_QK, RET_INNER, RET_INNER))
    z, xbc, dt, q, k, v, g = [w_in[:, o[i]:o[i + 1]] for i in range(7)]
    perm = _qk_perm()
    return jnp.concatenate([z, xbc, _pad_last(dt, LANES), q[:, perm], k[:, perm], v, g], axis=-1).astype(BF16)


def _pack_ffn_cols(a):
    return jnp.concatenate([_pad_last(a[..., :D_FF], D_FF_PAD), _pad_last(a[..., D_FF:], D_FF_PAD)], axis=-1)


def _rope_tables(seqlen):
    pos = jnp.arange(seqlen, dtype=F32)
    freqs = 1.0 / (ROPE_BASE ** jnp.linspace(0.0, 1.0, RET_QK_DIM // 2, dtype=F32))
    ang = pos[:, None] * freqs[None, :]
    cos, sin = jnp.cos(ang), jnp.sin(ang)
    return jnp.concatenate([cos, cos, cos, cos], axis=-1), jnp.concatenate([-sin, -sin, sin, sin], axis=-1)


def _pick_tile(seqlen, want):
    tile = min(want, seqlen)
    assert seqlen % tile == 0 and tile % CHUNK == 0
    return tile


def kernel(x, mem, w_in, ssd_conv_w, ssd_conv_b, ssd_dt_bias, ssd_a_log, ssd_d, ssd_norm_w, ret_gn_w, ret_gn_b, w_mix_out, ln1_g, ln1_b, w_xq, w_xk, w_xv, w_xo, ln2_g, ln2_b, w_ffn_up, b_ffn_up, ffn_conv_w, ffn_conv_b, w_ffn_down, ln3_g, ln3_b):
    seqlen = x.shape[1]
    tile = _pick_tile(seqlen, 256)
    cos_t, sin_t = _rope_tables(seqlen)
    row = lambda a: a.reshape(1, -1)
    for l in range(DEPTH):
        kt, vm = _memkv(mem, w_xk[l].astype(BF16), w_xv[l].astype(BF16))
        x = _mixer(
            x, cos_t, sin_t, _pack_in_proj(w_in[l]), ssd_conv_w[l], row(ssd_conv_b[l]),
            _pad_last(row(ssd_dt_bias[l]), LANES), _pad_last(row(ssd_a_log[l]), LANES),
            row(jnp.repeat(ssd_d[l], SSD_HEAD_DIM)), row(ssd_norm_w[l]), row(ret_gn_w[l]), row(ret_gn_b[l]),
            w_mix_out[l].astype(BF16), row(ln1_g[l]), row(ln1_b[l]), tile)
        x = _xattn(x, kt, vm, w_xq[l].astype(BF16), w_xo[l].astype(BF16), row(ln2_g[l]), row(ln2_b[l]), tile)
        x = _ffn(
            x, _pack_ffn_cols(w_ffn_up[l]).astype(BF16), row(_pack_ffn_cols(b_ffn_up[l])),
            _pack_ffn_cols(ffn_conv_w[l]), row(_pack_ffn_cols(ffn_conv_b[l])),
            _pad_last(w_ffn_down[l].T, D_FF_PAD).T.astype(BF16), row(ln3_g[l]), row(ln3_b[l]), tile)
    return x
```

```python
import functools

import numpy as np
import jax
import jax.numpy as jnp
from jax import lax
from jax.experimental import pallas as pl
from jax.experimental.pallas import tpu as pltpu

F32 = jnp.float32
BF16 = jnp.bfloat16

D_MODEL = 1024
DEPTH = 1
SSD_HEADS = 16
SSD_HEAD_DIM = 64
SSD_INNER = SSD_HEADS * SSD_HEAD_DIM
SSD_GROUPS = 2
SSD_STATE = 128
SSD_CONV = 4
SSD_CONV_DIM = SSD_INNER + 2 * SSD_GROUPS * SSD_STATE
GROUP_WIDTH = SSD_INNER // SSD_GROUPS
RET_HEADS = 8
RET_QK_DIM = 64
RET_V_DIM = 128
RET_QK = RET_HEADS * RET_QK_DIM
RET_INNER = RET_HEADS * RET_V_DIM
CHUNK = 128
ROPE_BASE = 10000.0
MIX_WIDTH = SSD_INNER + RET_INNER
XATTN_HEADS = 4
XATTN_HEAD_DIM = D_MODEL // XATTN_HEADS
D_FF = 2752
FFN_CONV = 3
ALPHA = (2.0 * DEPTH) ** 0.25
EPS = 1e-5

LANES = 128
SUBLANES = 8
VMEM_LIMIT_BYTES = 56 * 1024 * 1024

Z0 = 0
XBC0 = Z0 + SSD_INNER
DT0 = XBC0 + SSD_CONV_DIM
Q0 = DT0 + LANES
K0 = Q0 + RET_QK
V0 = K0 + RET_QK
G0 = V0 + RET_INNER
IN_PACKED = G0 + RET_INNER

D_FF_PAD = ((D_FF + LANES - 1) // LANES) * LANES
FFN_BLOCK = 256

MIXER_TILE = 256
XATTN_TILE = 256
FFN_TILE = 256
assert D_FF_PAD % FFN_BLOCK == 0


def _dot(a, b):
    return jnp.dot(a, b, preferred_element_type=F32)


def _dot_nt(a, b):
    return lax.dot_general(a, b, (((1,), (1,)), ((), ())), preferred_element_type=F32)


def _dot_tn(a, b):
    return lax.dot_general(a, b, (((0,), (0,)), ((), ())), preferred_element_type=F32)


def _silu(v):
    return v / (1.0 + jnp.exp(-v))


def _layer_norm(r, g, b):
    mu = jnp.mean(r, axis=-1, keepdims=True)
    rc = r - mu
    var = jnp.mean(rc * rc, axis=-1, keepdims=True)
    return rc * lax.rsqrt(var + EPS) * g + b


def _split2(w):
    hi = w.astype(BF16)
    lo = (w - hi.astype(F32)).astype(BF16)
    return hi, lo


def _split3(w):
    hi = w.astype(BF16)
    r1 = w - hi.astype(F32)
    mid = r1.astype(BF16)
    lo = (r1 - mid.astype(F32)).astype(BF16)
    return hi, mid, lo


def _ret_log_gamma():
    return np.log1p(-np.exp2(-5.0 - np.arange(RET_HEADS, dtype=np.float32))).astype(np.float32)


def _memkv_kernel(mem_ref, wk_ref, wv_ref, kt_ref, v_ref):
    m = mem_ref[0].astype(BF16)
    kt_ref[0] = lax.dot_general(wk_ref[...], m, (((0,), (1,)), ((), ())),
                                preferred_element_type=F32).astype(BF16)
    v_ref[0] = _dot(m, wv_ref[...]).astype(BF16)


def _memkv(mem, wk, wv):
    bsz, mlen, _ = mem.shape
    const = lambda b: (0, 0)
    return pl.pallas_call(
        _memkv_kernel,
        grid=(bsz,),
        in_specs=[
            pl.BlockSpec((1, mlen, D_MODEL), lambda b: (b, 0, 0)),
            pl.BlockSpec((D_MODEL, D_MODEL), const),
            pl.BlockSpec((D_MODEL, D_MODEL), const),
        ],
        out_specs=[
            pl.BlockSpec((1, D_MODEL, mlen), lambda b: (b, 0, 0)),
            pl.BlockSpec((1, mlen, D_MODEL), lambda b: (b, 0, 0)),
        ],
        out_shape=[
            jax.ShapeDtypeStruct((bsz, D_MODEL, mlen), BF16),
            jax.ShapeDtypeStruct((bsz, mlen, D_MODEL), BF16),
        ],
        compiler_params=pltpu.CompilerParams(
            dimension_semantics=("arbitrary",), vmem_limit_bytes=VMEM_LIMIT_BYTES),
        name="memkv",
    )(mem, wk, wv)


def _mixer_kernel(x_ref, cos_ref, sin_ref, win_ref, convw_ref, convb_ref, dtb_ref, alog_ref,
                  dskip_ref, normw_ref, gnw_ref, gnb_ref, wout_ref, lng_ref, lnb_ref,
                  out_ref,
                  z_s, xbc_s, xa_s, dt_s, q_s, k_s, v_s, g_s, ymix_s, hs_s, r_s,
                  dec_s, qd_s, kd_s, exp_s, tri_s, *, tile):
    b = pl.program_id(0)
    t = pl.program_id(1)
    log_gamma = _ret_log_gamma()

    @pl.when(jnp.logical_and(b == 0, t == 0))
    def _init_constants():
        row = lax.broadcasted_iota(jnp.int32, (CHUNK, CHUNK), 0)
        col = lax.broadcasted_iota(jnp.int32, (CHUNK, CHUNK), 1)
        tri_s[...] = jnp.where(row >= col, 1.0, 0.0).astype(BF16)
        erow = lax.broadcasted_iota(jnp.int32, (LANES, SSD_INNER), 0)
        ecol = lax.broadcasted_iota(jnp.int32, (LANES, SSD_INNER), 1)
        lo = erow * SSD_HEAD_DIM
        exp_s[...] = jnp.where(jnp.logical_and(ecol >= lo, ecol < lo + SSD_HEAD_DIM), 1.0, 0.0).astype(BF16)
        dist = (row - col).astype(F32)
        pos = row.astype(F32)
        half = RET_QK_DIM // 2
        for h in range(RET_HEADS):
            lg = float(log_gamma[h])
            in_head = jnp.bitwise_and(jnp.right_shift(col, 5), 1) == (h % 2)
            headmask = jnp.where(in_head, 1.0, 0.0)
            dec_s[h] = (RET_QK_DIM ** -0.5) * jnp.exp(jnp.where(dist >= 0, dist * lg, -jnp.inf))
            qd_s[h] = jnp.exp((pos + 1.0) * lg) * headmask
            kd_s[h] = (RET_QK_DIM ** -0.5) * jnp.exp((CHUNK - 1.0 - pos) * lg) * headmask

    @pl.when(t == 0)
    def _reset_state():
        hs_s[...] = jnp.zeros_like(hs_s)
        r_s[...] = jnp.zeros_like(r_s)
        xbc_s[0:SUBLANES, :] = jnp.zeros((SUBLANES, SSD_CONV_DIM), F32)

    x = x_ref[0]
    xb = x.astype(BF16)

    def proj(c0, n):
        return _dot(xb, win_ref[:, c0:c0 + n])

    z_s[...] = _silu(proj(Z0, SSD_INNER))
    xbc_s[SUBLANES:SUBLANES + tile, :] = proj(XBC0, SSD_CONV_DIM)
    dtr = proj(DT0, LANES) + dtb_ref[...]
    dt_s[...] = jnp.maximum(dtr, 0.0) + jnp.log1p(jnp.exp(-jnp.abs(dtr)))
    cos = cos_ref[...]
    sin = sin_ref[...]
    qraw = proj(Q0, RET_QK)
    kraw = proj(K0, RET_QK)
    for j in range(RET_QK // LANES):
        sl = slice(j * LANES, (j + 1) * LANES)
        qj = qraw[:, sl]
        kj = kraw[:, sl]
        q_s[:, sl] = qj * cos + pltpu.roll(qj, LANES // 2, 1) * sin
        k_s[:, sl] = kj * cos + pltpu.roll(kj, LANES // 2, 1) * sin
    v_s[...] = proj(V0, RET_INNER).astype(BF16)
    g_s[...] = _silu(proj(G0, RET_INNER))

    conv = convb_ref[...]
    for k in range(SSD_CONV):
        off = SUBLANES - (SSD_CONV - 1) + k
        conv = conv + convw_ref[k:k + 1, :] * xbc_s[off:off + tile, :]
    xa_s[...] = _silu(conv)
    xbc_s[0:SUBLANES, :] = xbc_s[tile:tile + SUBLANES, :]

    lane = lax.broadcasted_iota(jnp.int32, (1, LANES), 1)
    a_head = jnp.where(lane < SSD_HEADS, -jnp.exp(alog_ref[...]), 0.0)
    row = lax.broadcasted_iota(jnp.int32, (CHUNK, CHUNK), 0)
    col = lax.broadcasted_iota(jnp.int32, (CHUNK, CHUNK), 1)
    causal = row >= col
    first_half = col < SSD_HEAD_DIM
    tri = tri_s[...]
    pair_masks = []
    for j in range(2):
        pair_masks.append(jnp.where(jnp.bitwise_and(jnp.right_shift(col, 5), 1) == j, 1.0, 0.0))

    def expand(w):
        hi, lo = _split2(w)
        return _dot(hi, exp_s[...]) + _dot(lo, exp_s[...])

    def chunk_body(c, carry):
        r0 = pl.multiple_of(c * CHUNK, CHUNK)
        rows = pl.ds(r0, CHUNK)

        dt = dt_s[rows, :]
        a = dt * a_head
        a_hi, a_mid, a_lo = _split3(a)
        acs = _dot(tri, a_hi) + _dot(tri, a_mid) + _dot(tri, a_lo)
        tot = acs[CHUNK - 1:CHUNK, :]
        w1x = expand(dt * jnp.exp(tot - acs))
        dfsx = expand(jnp.exp(acs))
        cdx = expand(jnp.broadcast_to(jnp.exp(tot), (SUBLANES, LANES)))[0:1, :]
        acs_t = acs.T
        dt_t = dt.T
        for g in range(SSD_GROUPS):
            gs = slice(g * GROUP_WIDTH, (g + 1) * GROUP_WIDTH)
            b0 = SSD_INNER + g * SSD_STATE
            c0 = SSD_INNER + SSD_GROUPS * SSD_STATE + g * SSD_STATE
            bm = xa_s[rows, b0:b0 + SSD_STATE].astype(BF16)
            cm = xa_s[rows, c0:c0 + SSD_STATE].astype(BF16)
            xg = xa_s[rows, gs]
            cb = _dot_nt(cm, bm)
            h_prev = hs_s[g]
            y_off = _dot(cm, h_prev.astype(BF16)) * dfsx[:, gs]
            states = _dot_tn(bm, (xg * w1x[:, gs]).astype(BF16))
            hs_s[g] = h_prev * cdx[:, gs] + states
            us = []
            for pr in range(GROUP_WIDTH // LANES):
                att = []
                for j in range(2):
                    h = g * (SSD_HEADS // SSD_GROUPS) + 2 * pr + j
                    seg = jnp.exp(jnp.where(causal, acs[:, h:h + 1] - acs_t[h:h + 1, :], -jnp.inf))
                    att.append((cb * seg * dt_t[h:h + 1, :]).astype(BF16))
                ps = slice(pr * LANES, (pr + 1) * LANES)
                xp = xg[:, ps]
                rhs = jnp.concatenate([jnp.where(first_half, xp, 0.0).astype(BF16),
                                       jnp.where(first_half, 0.0, xp).astype(BF16)], axis=0)
                y = _dot(jnp.concatenate(att, axis=1), rhs)
                gl = slice(g * GROUP_WIDTH + pr * LANES, g * GROUP_WIDTH + (pr + 1) * LANES)
                y = y + y_off[:, ps] + dskip_ref[:, gl] * xp
                us.append(y * z_s[rows, gl])
            ssq = us[0] * us[0]
            for u in us[1:]:
                ssq = ssq + u * u
            scale = lax.rsqrt(jnp.sum(ssq, axis=-1, keepdims=True) * (1.0 / GROUP_WIDTH) + EPS)
            for pr, u in enumerate(us):
                gl = slice(g * GROUP_WIDTH + pr * LANES, g * GROUP_WIDTH + (pr + 1) * LANES)
                ymix_s[rows, gl] = (u * scale * normw_ref[:, gl]).astype(BF16)

        for pr in range(RET_HEADS // 2):
            ps = slice(pr * LANES, (pr + 1) * LANES)
            qp = q_s[rows, ps]
            kp = k_s[rows, ps]
            kpb = kp.astype(BF16)
            for j in range(2):
                h = 2 * pr + j
                hl = slice(h * RET_V_DIM, (h + 1) * RET_V_DIM)
                s = _dot_nt((qp * pair_masks[j]).astype(BF16), kpb) * dec_s[h]
                qd = (qp * qd_s[h]).astype(BF16)
                vh = v_s[rows, hl]
                r_prev = r_s[h]
                y = _dot(jnp.concatenate([s.astype(BF16), qd], axis=1),
                         jnp.concatenate([vh, r_prev.astype(BF16)], axis=0))
                r_s[h] = r_prev * float(np.exp(np.float32(CHUNK) * log_gamma[h])) + _dot_tn(
                    (kp * kd_s[h]).astype(BF16), vh)
                mu = jnp.mean(y, axis=-1, keepdims=True)
                yc = y - mu
                var = jnp.mean(yc * yc, axis=-1, keepdims=True)
                yn = yc * lax.rsqrt(var + EPS) * gnw_ref[:, hl] + gnb_ref[:, hl]
                ymix_s[rows, SSD_INNER + h * RET_V_DIM:SSD_INNER + (h + 1) * RET_V_DIM] = (
                    g_s[rows, hl] * yn).astype(BF16)
        return carry

    lax.fori_loop(0, tile // CHUNK, chunk_body, 0)

    mix = _dot(ymix_s[...], wout_ref[...])
    out_ref[0] = _layer_norm(ALPHA * x + mix, lng_ref[...], lnb_ref[...])


def _mixer(x, cos_t, sin_t, w_in_p, conv_w, conv_b, dtb, alog, dskip, normw, gnw, gnb, w_out, ln_g, ln_b, tile):
    bsz, seqlen, _ = x.shape
    nt = seqlen // tile
    const = lambda b, t: (0, 0)

    def cspec(shape):
        return pl.BlockSpec(shape, const, pipeline_mode=pl.Buffered(1))

    return pl.pallas_call(
        functools.partial(_mixer_kernel, tile=tile),
        grid=(bsz, nt),
        in_specs=[
            pl.BlockSpec((1, tile, D_MODEL), lambda b, t: (b, t, 0)),
            pl.BlockSpec((tile, LANES), lambda b, t: (t, 0)),
            pl.BlockSpec((tile, LANES), lambda b, t: (t, 0)),
            cspec((D_MODEL, IN_PACKED)),
            cspec((SSD_CONV, SSD_CONV_DIM)),
            cspec((1, SSD_CONV_DIM)),
            cspec((1, LANES)),
            cspec((1, LANES)),
            cspec((1, SSD_INNER)),
            cspec((1, SSD_INNER)),
            cspec((1, RET_INNER)),
            cspec((1, RET_INNER)),
            cspec((MIX_WIDTH, D_MODEL)),
            cspec((1, D_MODEL)),
            cspec((1, D_MODEL)),
        ],
        out_specs=pl.BlockSpec((1, tile, D_MODEL), lambda b, t: (b, t, 0)),
        out_shape=jax.ShapeDtypeStruct((bsz, seqlen, D_MODEL), F32),
        scratch_shapes=[
            pltpu.VMEM((tile, SSD_INNER), F32),
            pltpu.VMEM((tile + SUBLANES, SSD_CONV_DIM), F32),
            pltpu.VMEM((tile, SSD_CONV_DIM), F32),
            pltpu.VMEM((tile, LANES), F32),
            pltpu.VMEM((tile, RET_QK), F32),
            pltpu.VMEM((tile, RET_QK), F32),
            pltpu.VMEM((tile, RET_INNER), BF16),
            pltpu.VMEM((tile, RET_INNER), F32),
            pltpu.VMEM((tile, MIX_WIDTH), BF16),
            pltpu.VMEM((SSD_GROUPS, SSD_STATE, GROUP_WIDTH), F32),
            pltpu.VMEM((RET_HEADS, LANES, RET_V_DIM), F32),
            pltpu.VMEM((RET_HEADS, CHUNK, CHUNK), F32),
            pltpu.VMEM((RET_HEADS, CHUNK, LANES), F32),
            pltpu.VMEM((RET_HEADS, CHUNK, LANES), F32),
            pltpu.VMEM((LANES, SSD_INNER), BF16),
            pltpu.VMEM((CHUNK, CHUNK), BF16),
        ],
        compiler_params=pltpu.CompilerParams(
            dimension_semantics=("arbitrary", "arbitrary"), vmem_limit_bytes=VMEM_LIMIT_BYTES),
        name="mixer",
    )(x, cos_t, sin_t, w_in_p, conv_w, conv_b, dtb, alog, dskip, normw, gnw, gnb, w_out, ln_g, ln_b)


def _xattn_kernel(x_ref, kt_ref, v_ref, wq_ref, wo_ref, lng_ref, lnb_ref, out_ref, o_s):
    x = x_ref[0]
    q = _dot(x.astype(BF16), wq_ref[...]).astype(BF16)
    for h in range(XATTN_HEADS):
        hs = slice(h * XATTN_HEAD_DIM, (h + 1) * XATTN_HEAD_DIM)
        s = _dot(q[:, hs], kt_ref[0, hs, :]) * (XATTN_HEAD_DIM ** -0.5)
        m = jnp.max(s, axis=-1, keepdims=True)
        p = jnp.exp(s - m)
        p = p / jnp.sum(p, axis=-1, keepdims=True)
        o_s[:, hs] = _dot(p.astype(BF16), v_ref[0, :, hs]).astype(BF16)
    xa = _dot(o_s[...], wo_ref[...])
    out_ref[0] = _layer_norm(ALPHA * x + xa, lng_ref[...], lnb_ref[...])


def _xattn(x, kt, vm, wq, wo, ln_g, ln_b, tile):
    bsz, seqlen, _ = x.shape
    mlen = vm.shape[1]
    const = lambda b, t: (0, 0)

    def cspec(shape):
        return pl.BlockSpec(shape, const, pipeline_mode=pl.Buffered(1))

    return pl.pallas_call(
        _xattn_kernel,
        grid=(bsz, seqlen // tile),
        in_specs=[
            pl.BlockSpec((1, tile, D_MODEL), lambda b, t: (b, t, 0)),
            pl.BlockSpec((1, D_MODEL, mlen), lambda b, t: (b, 0, 0)),
            pl.BlockSpec((1, mlen, D_MODEL), lambda b, t: (b, 0, 0)),
            cspec((D_MODEL, D_MODEL)),
            cspec((D_MODEL, D_MODEL)),
            cspec((1, D_MODEL)),
            cspec((1, D_MODEL)),
        ],
        out_specs=pl.BlockSpec((1, tile, D_MODEL), lambda b, t: (b, t, 0)),
        out_shape=jax.ShapeDtypeStruct((bsz, seqlen, D_MODEL), F32),
        scratch_shapes=[pltpu.VMEM((tile, D_MODEL), BF16)],
        compiler_params=pltpu.CompilerParams(
            dimension_semantics=("arbitrary", "arbitrary"), vmem_limit_bytes=VMEM_LIMIT_BYTES),
        name="xattn",
    )(x, kt, vm, wq, wo, ln_g, ln_b)


def _ffn_kernel(x_ref, wup_ref, bup_ref, convw_ref, convb_ref, wdown_ref, lng_ref, lnb_ref,
                out_ref, acc_s, *ext, tile):
    t = pl.program_id(1)
    tail = slice(tile, tile + SUBLANES)
    nblk = D_FF_PAD // FFN_BLOCK

    @pl.when(t == 0)
    def _reset_halo():
        for e in ext:
            e[tail, :] = jnp.zeros((SUBLANES, FFN_BLOCK), F32)

    x = x_ref[0]
    xb = x.astype(BF16)

    def up(i):
        cs = slice(i * FFN_BLOCK, (i + 1) * FFN_BLOCK)
        ext[i][0:SUBLANES, :] = ext[i][tail, :]
        ext[i][SUBLANES:SUBLANES + tile, :] = _dot(xb, wup_ref[:, cs]) + bup_ref[:, cs]

    def conv(i):
        cs = slice(i * FFN_BLOCK, (i + 1) * FFN_BLOCK)
        y = convb_ref[:, cs]
        for k in range(FFN_CONV):
            off = SUBLANES - (FFN_CONV - 1) + k
            y = y + convw_ref[k:k + 1, cs] * ext[i][off:off + tile, :]
        return y

    def down(j, gated):
        part = _dot(gated, wdown_ref[j * FFN_BLOCK:(j + 1) * FFN_BLOCK, :])
        if j == 0:
            acc_s[...] = part
        else:
            acc_s[...] += part

    ahead = 2
    for i in range(ahead):
        up(i)
        up(nblk + i)
    gated_prev = None
    for j in range(nblk):
        if j + ahead < nblk:
            up(j + ahead)
            up(nblk + j + ahead)
        if gated_prev is not None:
            down(j - 1, gated_prev)
        gated_prev = (_silu(conv(j)) * conv(nblk + j)).astype(BF16)
    down(nblk - 1, gated_prev)
    out_ref[0] = _layer_norm(ALPHA * x + acc_s[...], lng_ref[...], lnb_ref[...])


def _ffn(x, w_up, b_up, conv_w, conv_b, w_down, ln_g, ln_b, tile):
    bsz, seqlen, _ = x.shape
    const = lambda b, t: (0, 0)

    def cspec(shape):
        return pl.BlockSpec(shape, const, pipeline_mode=pl.Buffered(1))

    return pl.pallas_call(
        functools.partial(_ffn_kernel, tile=tile),
        grid=(bsz, seqlen // tile),
        in_specs=[
            pl.BlockSpec((1, tile, D_MODEL), lambda b, t: (b, t, 0)),
            cspec((D_MODEL, 2 * D_FF_PAD)),
            cspec((1, 2 * D_FF_PAD)),
            cspec((FFN_CONV, 2 * D_FF_PAD)),
            cspec((1, 2 * D_FF_PAD)),
            cspec((D_FF_PAD, D_MODEL)),
            cspec((1, D_MODEL)),
            cspec((1, D_MODEL)),
        ],
        out_specs=pl.BlockSpec((1, tile, D_MODEL), lambda b, t: (b, t, 0)),
        out_shape=jax.ShapeDtypeStruct((bsz, seqlen, D_MODEL), F32),
        scratch_shapes=[pltpu.VMEM((tile, D_MODEL), F32)] + [
            pltpu.VMEM((tile + SUBLANES, FFN_BLOCK), F32) for _ in range(2 * D_FF_PAD // FFN_BLOCK)],
        compiler_params=pltpu.CompilerParams(
            dimension_semantics=("arbitrary", "arbitrary"), vmem_limit_bytes=VMEM_LIMIT_BYTES),
        name="ffn",
    )(x, w_up, b_up, conv_w, conv_b, w_down, ln_g, ln_b)


def _qk_perm():
    half = RET_QK_DIM // 2
    idx = []
    for pr in range(RET_HEADS // 2):
        for part in range(2):
            for j in range(2):
                h = 2 * pr + j
                idx.extend(range(h * RET_QK_DIM + part * half, h * RET_QK_DIM + (part + 1) * half))
    return np.asarray(idx, dtype=np.int32)


def _pad_last(a, n):
    return jnp.pad(a, [(0, 0)] * (a.ndim - 1) + [(0, n - a.shape[-1])])


def _pack_in_proj(w_in):
    o = np.cumsum((0, SSD_INNER, SSD_CONV_DIM, SSD_HEADS, RET_QK, RET_QK, RET_INNER, RET_INNER))
    z, xbc, dt, q, k, v, g = [w_in[:, o[i]:o[i + 1]] for i in range(7)]
    perm = _qk_perm()
    return jnp.concatenate([z, xbc, _pad_last(dt, LANES), q[:, perm], k[:, perm], v, g], axis=-1).astype(BF16)


def _pack_ffn_cols(a):
    return jnp.concatenate([_pad_last(a[..., :D_FF], D_FF_PAD), _pad_last(a[..., D_FF:], D_FF_PAD)], axis=-1)


def _rope_tables(seqlen):
    pos = jnp.arange(seqlen, dtype=F32)
    freqs = 1.0 / (ROPE_BASE ** jnp.linspace(0.0, 1.0, RET_QK_DIM // 2, dtype=F32))
    ang = pos[:, None] * freqs[None, :]
    cos, sin = jnp.cos(ang), jnp.sin(ang)
    return jnp.concatenate([cos, cos, cos, cos], axis=-1), jnp.concatenate([-sin, -sin, sin, sin], axis=-1)


def _pick_tile(seqlen, want):
    tile = min(want, seqlen)
    assert seqlen % tile == 0 and tile % CHUNK == 0
    return tile


def kernel(x, mem, w_in, ssd_conv_w, ssd_conv_b, ssd_dt_bias, ssd_a_log, ssd_d, ssd_norm_w, ret_gn_w, ret_gn_b, w_mix_out, ln1_g, ln1_b, w_xq, w_xk, w_xv, w_xo, ln2_g, ln2_b, w_ffn_up, b_ffn_up, ffn_conv_w, ffn_conv_b, w_ffn_down, ln3_g, ln3_b):
    seqlen = x.shape[1]
    tile = _pick_tile(seqlen, MIXER_TILE)
    xattn_tile = _pick_tile(seqlen, XATTN_TILE)
    ffn_tile = _pick_tile(seqlen, FFN_TILE)
    cos_t, sin_t = _rope_tables(seqlen)
    row = lambda a: a.reshape(1, -1)
    for l in range(DEPTH):
        kt, vm = _memkv(mem, w_xk[l].astype(BF16), w_xv[l].astype(BF16))
        x = _mixer(
            x, cos_t, sin_t, _pack_in_proj(w_in[l]), ssd_conv_w[l], row(ssd_conv_b[l]),
            _pad_last(row(ssd_dt_bias[l]), LANES), _pad_last(row(ssd_a_log[l]), LANES),
            row(jnp.repeat(ssd_d[l], SSD_HEAD_DIM)), row(ssd_norm_w[l]), row(ret_gn_w[l]), row(ret_gn_b[l]),
            w_mix_out[l].astype(BF16), row(ln1_g[l]), row(ln1_b[l]), tile)
        x = _xattn(x, kt, vm, w_xq[l].astype(BF16), w_xo[l].astype(BF16), row(ln2_g[l]), row(ln2_b[l]),
                   xattn_tile)
        x = _ffn(
            x, _pack_ffn_cols(w_ffn_up[l]).astype(BF16), row(_pack_ffn_cols(b_ffn_up[l])),
            _pack_ffn_cols(ffn_conv_w[l]), row(_pack_ffn_cols(ffn_conv_b[l])),
            _pad_last(w_ffn_down[l].T, D_FF_PAD).T.astype(BF16), row(ln3_g[l]), row(ln3_b[l]), ffn_tile)
    return x
```

```python
import functools

import numpy as np
import jax
import jax.numpy as jnp
from jax import lax
from jax.experimental import pallas as pl
from jax.experimental.pallas import tpu as pltpu

F32 = jnp.float32
BF16 = jnp.bfloat16

D_MODEL = 1024
DEPTH = 1
SSD_HEADS = 16
SSD_HEAD_DIM = 64
SSD_INNER = SSD_HEADS * SSD_HEAD_DIM
SSD_GROUPS = 2
SSD_STATE = 128
SSD_CONV = 4
SSD_CONV_DIM = SSD_INNER + 2 * SSD_GROUPS * SSD_STATE
GROUP_WIDTH = SSD_INNER // SSD_GROUPS
RET_HEADS = 8
RET_QK_DIM = 64
RET_V_DIM = 128
RET_QK = RET_HEADS * RET_QK_DIM
RET_INNER = RET_HEADS * RET_V_DIM
CHUNK = 128
ROPE_BASE = 10000.0
MIX_WIDTH = SSD_INNER + RET_INNER
XATTN_HEADS = 4
XATTN_HEAD_DIM = D_MODEL // XATTN_HEADS
D_FF = 2752
FFN_CONV = 3
ALPHA = (2.0 * DEPTH) ** 0.25
EPS = 1e-5

LANES = 128
SUBLANES = 8
VMEM_LIMIT_BYTES = 56 * 1024 * 1024
MIXER_VMEM_LIMIT_BYTES = 60 * 1024 * 1024

Z0 = 0
XBC0 = Z0 + SSD_INNER
DT0 = XBC0 + SSD_CONV_DIM
Q0 = DT0 + LANES
K0 = Q0 + RET_QK
V0 = K0 + RET_QK
G0 = V0 + RET_INNER
IN_PACKED = G0 + RET_INNER

D_FF_PAD = ((D_FF + LANES - 1) // LANES) * LANES
FFN_BLOCK = 256
PROJ_BLOCK = 256
OUT_BLOCK = 256
assert D_FF_PAD % FFN_BLOCK == 0 and DT0 % PROJ_BLOCK == 0 and (IN_PACKED - Q0) % PROJ_BLOCK == 0

MIXER_TILE = 256
XATTN_TILE = 256
FFN_TILE = 256


def _dot(a, b):
    return jnp.dot(a, b, preferred_element_type=F32)


def _dot_nt(a, b):
    return lax.dot_general(a, b, (((1,), (1,)), ((), ())), preferred_element_type=F32)


def _dot_tn(a, b):
    return lax.dot_general(a, b, (((0,), (0,)), ((), ())), preferred_element_type=F32)


def _silu(v):
    return v / (1.0 + jnp.exp(-v))


def _layer_norm(r, g, b):
    mu = jnp.mean(r, axis=-1, keepdims=True)
    rc = r - mu
    var = jnp.mean(rc * rc, axis=-1, keepdims=True)
    return rc * lax.rsqrt(var + EPS) * g + b


def _split2(w):
    hi = w.astype(BF16)
    lo = (w - hi.astype(F32)).astype(BF16)
    return hi, lo


def _split3(w):
    hi = w.astype(BF16)
    r1 = w - hi.astype(F32)
    mid = r1.astype(BF16)
    lo = (r1 - mid.astype(F32)).astype(BF16)
    return hi, mid, lo


def _ret_log_gamma():
    return np.log1p(-np.exp2(-5.0 - np.arange(RET_HEADS, dtype=np.float32))).astype(np.float32)


def _pair_member(lane_idx):
    half = RET_QK_DIM // 2
    return jnp.bitwise_and(jnp.right_shift(lane_idx, half.bit_length() - 1), 1)


def _memkv_kernel(mem_ref, wk_ref, wv_ref, kt_ref, v_ref):
    m = mem_ref[0].astype(BF16)
    kt_ref[0] = lax.dot_general(wk_ref[...], m, (((0,), (1,)), ((), ())),
                                preferred_element_type=F32).astype(BF16)
    v_ref[0] = _dot(m, wv_ref[...]).astype(BF16)


def _memkv(mem, wk, wv):
    bsz, mlen, _ = mem.shape
    const = lambda b: (0, 0)
    return pl.pallas_call(
        _memkv_kernel,
        grid=(bsz,),
        in_specs=[
            pl.BlockSpec((1, mlen, D_MODEL), lambda b: (b, 0, 0)),
            pl.BlockSpec((D_MODEL, D_MODEL), const),
            pl.BlockSpec((D_MODEL, D_MODEL), const),
        ],
        out_specs=[
            pl.BlockSpec((1, D_MODEL, mlen), lambda b: (b, 0, 0)),
            pl.BlockSpec((1, mlen, D_MODEL), lambda b: (b, 0, 0)),
        ],
        out_shape=[
            jax.ShapeDtypeStruct((bsz, D_MODEL, mlen), BF16),
            jax.ShapeDtypeStruct((bsz, mlen, D_MODEL), BF16),
        ],
        compiler_params=pltpu.CompilerParams(
            dimension_semantics=("arbitrary",), vmem_limit_bytes=VMEM_LIMIT_BYTES),
        name="memkv",
    )(mem, wk, wv)


def _proj_blocks():
    blocks = [(c, PROJ_BLOCK) for c in range(0, DT0, PROJ_BLOCK)]
    blocks.append((DT0, LANES))
    blocks += [(c, PROJ_BLOCK) for c in range(Q0, IN_PACKED, PROJ_BLOCK)]
    return blocks


def _mixer_kernel(xp_ref, xr_ref, cos_a_ref, sin_a_ref, cos_b_ref, sin_b_ref,
                  win_ref, convw_ref, convb_ref, dtb_ref, alog_ref,
                  dskip_ref, normw_ref, gnw_ref, gnb_ref, wout_ref, lng_ref, lnb_ref,
                  out_ref,
                  proj_a, proj_b, ymix_a, ymix_b, xa_a, xa_b, dt_a, dt_b, q_a, q_b, k_a, k_b,
                  halo_s, hs_s, r_s, dec_s, qd_s, kd_s, exp_s, tri_s, *, tile, nt, ntiles):
    m = pl.program_id(0)
    log_gamma = _ret_log_gamma()
    top = SUBLANES

    @pl.when(m == 0)
    def _init():
        proj_b[...] = jnp.zeros_like(proj_b)
        ymix_a[...] = jnp.zeros_like(ymix_a)
        hs_s[...] = jnp.zeros_like(hs_s)
        r_s[...] = jnp.zeros_like(r_s)
        halo_s[...] = jnp.zeros_like(halo_s)
        row = lax.broadcasted_iota(jnp.int32, (CHUNK, CHUNK), 0)
        col = lax.broadcasted_iota(jnp.int32, (CHUNK, CHUNK), 1)
        tri_s[...] = jnp.where(row >= col, 1.0, 0.0).astype(BF16)
        erow = lax.broadcasted_iota(jnp.int32, (LANES, 2 * SSD_INNER), 0)
        ecol = lax.broadcasted_iota(jnp.int32, (LANES, 2 * SSD_INNER), 1)
        slot = erow // SSD_HEADS
        lo = (slot // 2) * SSD_INNER + (erow % SSD_HEADS) * SSD_HEAD_DIM
        hit = jnp.logical_and(slot < 4, jnp.logical_and(ecol >= lo, ecol < lo + SSD_HEAD_DIM))
        exp_s[...] = jnp.where(hit, 1.0, 0.0).astype(BF16)
        dist = (row - col).astype(F32)
        pos = row.astype(F32)
        for h in range(RET_HEADS):
            lg = float(log_gamma[h])
            headmask = jnp.where(_pair_member(col) == (h % 2), 1.0, 0.0)
            dec_s[h] = (RET_QK_DIM ** -0.5) * jnp.exp(jnp.where(dist >= 0, dist * lg, -jnp.inf))
            qd_s[h] = jnp.exp((pos + 1.0) * lg) * headmask
            kd_s[h] = (RET_QK_DIM ** -0.5) * jnp.exp((CHUNK - 1.0 - pos) * lg)

    lane = lax.broadcasted_iota(jnp.int32, (1, LANES), 1)
    a_head = jnp.where(lane < SSD_HEADS, -jnp.exp(alog_ref[...]), 0.0)
    row = lax.broadcasted_iota(jnp.int32, (CHUNK, CHUNK), 0)
    col = lax.broadcasted_iota(jnp.int32, (CHUNK, CHUNK), 1)
    causal = row >= col
    first_half = col < SSD_HEAD_DIM
    pair_masks = [jnp.where(_pair_member(col) == j, 1.0, 0.0) for j in range(2)]

    head_lanes = lane < SSD_HEADS

    def pack_heads(parts):
        out = None
        for i, part in enumerate(parts):
            part = jnp.where(head_lanes, part, 0.0)
            if i:
                part = pltpu.roll(part, SSD_HEADS * i, 1)
            out = part if out is None else out + part
        return out

    def unpack_heads_sum(packed, nparts):
        out = packed
        for i in range(1, nparts):
            out = out + pltpu.roll(packed, LANES - SSD_HEADS * i, 1)
        return out

    def split2_f32(w):
        hi, lo = _split2(w)
        return [hi.astype(F32), lo.astype(F32)]

    def half(x_rows, c_tile, cos_ref, sin_ref, pp, pc, ymix_c, ymix_o, xa_s, dt_s, q_s, k_s):
        fresh = lax.rem(c_tile, nt) == 0

        xb = xp_ref[0, x_rows, :].astype(BF16)

        def p_task(c0, width):
            def run():
                pp[top:top + tile, c0:c0 + width] = _dot(xb, win_ref[:, c0:c0 + width])
            return run

        p_tasks = [p_task(c0, width) for c0, width in _proj_blocks()]

        def o_task(j):
            def run():
                cs = slice(j * OUT_BLOCK, (j + 1) * OUT_BLOCK)
                out_ref[0, x_rows, cs] = ALPHA * xr_ref[0, x_rows, cs] + _dot(ymix_o[...], wout_ref[:, cs])
            return run

        def o_finish():
            out_ref[0, x_rows, :] = _layer_norm(out_ref[0, x_rows, :], lng_ref[...], lnb_ref[...])

        o_tasks = [o_task(j) for j in range(D_MODEL // OUT_BLOCK)]

        def conv_task(j):
            def run():
                cs = slice(j * PROJ_BLOCK, (j + 1) * PROJ_BLOCK)
                ps = slice(XBC0 + j * PROJ_BLOCK, XBC0 + (j + 1) * PROJ_BLOCK)
                pc[0:SUBLANES, ps] = jnp.where(fresh, 0.0, halo_s[:, cs])
                conv = convb_ref[:, cs]
                for k in range(SSD_CONV):
                    off = top - (SSD_CONV - 1) + k
                    conv = conv + convw_ref[k:k + 1, cs] * pc[off:off + tile, ps]
                xa_s[:, cs] = _silu(conv)
                halo_s[:, cs] = pc[tile:tile + SUBLANES, ps]
            return run

        def dt_task():
            dtr = pc[top:top + tile, DT0:DT0 + LANES] + dtb_ref[...]
            dt_s[...] = jnp.maximum(dtr, 0.0) + jnp.log1p(jnp.exp(-jnp.abs(dtr)))

        def rot_task(j):
            def run():
                cos = cos_ref[...]
                sin = sin_ref[...]
                qj = pc[top:top + tile, Q0 + j * LANES:Q0 + (j + 1) * LANES]
                kj = pc[top:top + tile, K0 + j * LANES:K0 + (j + 1) * LANES]
                sl = slice(j * LANES, (j + 1) * LANES)
                q_s[:, sl] = qj * cos + pltpu.roll(qj, LANES // 2, 1) * sin
                k_s[:, sl] = kj * cos + pltpu.roll(kj, LANES // 2, 1) * sin
            return run

        def chunk_tasks(c):
            rows = slice(c * CHUNK, (c + 1) * CHUNK)
            prow = slice(top + c * CHUNK, top + (c + 1) * CHUNK)
            shared = {}

            def state(ref_value):
                return jnp.where(fresh, 0.0, ref_value) if c == 0 else ref_value

            def pre():
                dt = dt_s[rows, :]
                a = dt * a_head
                a_parts = pack_heads([p.astype(F32) for p in _split3(a)]).astype(BF16)
                acs = unpack_heads_sum(_dot(tri_s[...], a_parts), 3)
                tot = acs[CHUNK - 1:CHUNK, :]
                shared["acs"] = acs
                w1 = dt * jnp.exp(tot - acs)
                dfs = jnp.exp(acs)
                both = _dot(pack_heads(split2_f32(w1) + split2_f32(dfs)).astype(BF16), exp_s[...])
                shared["w1x"] = both[:, 0:SSD_INNER]
                shared["dfsx"] = both[:, SSD_INNER:2 * SSD_INNER]
                cd = jnp.broadcast_to(jnp.exp(tot), (SUBLANES, LANES))
                shared["cdx"] = _dot(pack_heads(split2_f32(cd)).astype(BF16), exp_s[:, 0:SSD_INNER])[0:1, :]
                shared["acs_t"] = acs.T
                shared["dt_t"] = dt.T

            def ssd_tasks(g):
                gs = slice(g * GROUP_WIDTH, (g + 1) * GROUP_WIDTH)
                b0 = SSD_INNER + g * SSD_STATE
                c0 = SSD_INNER + SSD_GROUPS * SSD_STATE + g * SSD_STATE
                grp = {"us": []}

                def head():
                    bm = xa_s[rows, b0:b0 + SSD_STATE].astype(BF16)
                    cm = xa_s[rows, c0:c0 + SSD_STATE].astype(BF16)
                    grp["cb"] = _dot_nt(cm, bm)
                    h_prev = state(hs_s[g])
                    grp["y_off"] = _dot(cm, h_prev.astype(BF16)) * shared["dfsx"][:, gs]
                    states = _dot_tn(bm, (xa_s[rows, gs] * shared["w1x"][:, gs]).astype(BF16))
                    hs_s[g] = h_prev * shared["cdx"][:, gs] + states

                def pair(pr):
                    def run():
                        acs, acs_t, dt_t = shared["acs"], shared["acs_t"], shared["dt_t"]
                        att = []
                        for j in range(2):
                            h = g * (SSD_HEADS // SSD_GROUPS) + 2 * pr + j
                            seg = jnp.exp(jnp.where(causal, acs[:, h:h + 1] - acs_t[h:h + 1, :], -jnp.inf))
                            att.append((grp["cb"] * seg * dt_t[h:h + 1, :]).astype(BF16))
                        ps = slice(pr * LANES, (pr + 1) * LANES)
                        gl = slice(g * GROUP_WIDTH + pr * LANES, g * GROUP_WIDTH + (pr + 1) * LANES)
                        xp = xa_s[rows, gl]
                        rhs = jnp.concatenate([jnp.where(first_half, xp, 0.0).astype(BF16),
                                               jnp.where(first_half, 0.0, xp).astype(BF16)], axis=0)
                        y = _dot(jnp.concatenate(att, axis=1), rhs)
                        y = y + grp["y_off"][:, ps] + dskip_ref[:, gl] * xp
                        grp["us"].append(y * _silu(pc[prow, Z0 + gl.start:Z0 + gl.stop]))
                    return run

                def norm():
                    us = grp["us"]
                    ssq = us[0] * us[0]
                    for u in us[1:]:
                        ssq = ssq + u * u
                    scale = lax.rsqrt(jnp.sum(ssq, axis=-1, keepdims=True) * (1.0 / GROUP_WIDTH) + EPS)
                    for pr, u in enumerate(us):
                        gl = slice(g * GROUP_WIDTH + pr * LANES, g * GROUP_WIDTH + (pr + 1) * LANES)
                        ymix_c[rows, gl] = (u * scale * normw_ref[:, gl]).astype(BF16)

                return [head] + [pair(pr) for pr in range(GROUP_WIDTH // LANES)] + [norm]

            def ret_scores(pr):
                def run():
                    ps = slice(pr * LANES, (pr + 1) * LANES)
                    qp = q_s[rows, ps]
                    kp = k_s[rows, ps]
                    kcat = jnp.concatenate([(kp * pair_masks[j]).astype(BF16) for j in range(2)], axis=0)
                    pair[pr] = {"s2": _dot_nt(qp.astype(BF16), kcat)}
                    vk = []
                    for j in range(2):
                        h = 2 * pr + j
                        vk.append((pc[prow, V0 + h * RET_V_DIM:V0 + (h + 1) * RET_V_DIM] * kd_s[h]).astype(BF16))
                    pair[pr]["st2"] = _dot_tn(kp.astype(BF16), jnp.concatenate(vk, axis=1))
                return run

            def ret_task(h):
                def run():
                    pr, j = divmod(h, 2)
                    ps = slice(pr * LANES, (pr + 1) * LANES)
                    js = slice(j * RET_V_DIM, (j + 1) * RET_V_DIM)
                    qp = q_s[rows, ps]
                    hl = slice(h * RET_V_DIM, (h + 1) * RET_V_DIM)
                    s = pair[pr]["s2"][:, js] * dec_s[h]
                    qd = (qp * qd_s[h]).astype(BF16)
                    vh = pc[prow, V0 + hl.start:V0 + hl.stop].astype(BF16)
                    r_prev = state(r_s[h])
                    y = _dot(jnp.concatenate([s.astype(BF16), qd], axis=1),
                             jnp.concatenate([vh, r_prev.astype(BF16)], axis=0))
                    r_s[h] = r_prev * float(np.exp(np.float32(CHUNK) * log_gamma[h])) + pair[pr]["st2"][:, js]
                    mu = jnp.mean(y, axis=-1, keepdims=True)
                    yc = y - mu
                    var = jnp.mean(yc * yc, axis=-1, keepdims=True)
                    yn = yc * lax.rsqrt(var + EPS) * gnw_ref[:, hl] + gnb_ref[:, hl]
                    gate = _silu(pc[prow, G0 + hl.start:G0 + hl.stop])
                    ymix_c[rows, SSD_INNER + hl.start:SSD_INNER + hl.stop] = (gate * yn).astype(BF16)
                return run

            pair = {}
            tasks = [pre]
            for g in range(SSD_GROUPS):
                tasks += ssd_tasks(g)
            for pr in range(RET_HEADS // 2):
                tasks += [ret_scores(pr), ret_task(2 * pr), ret_task(2 * pr + 1)]
            return tasks

        c_tasks = [conv_task(j) for j in range(SSD_CONV_DIM // PROJ_BLOCK)] + [dt_task]
        c_tasks += [rot_task(j) for j in range(RET_QK // LANES)]
        for c in range(tile // CHUNK):
            c_tasks += chunk_tasks(c)

        m_tasks = p_tasks + o_tasks
        done = 0
        for i, task in enumerate(c_tasks):
            task()
            due = ((i + 1) * len(m_tasks)) // len(c_tasks)
            for mt in m_tasks[done:due]:
                mt()
            done = due
        o_finish()

    last = ntiles - 1
    half(slice(0, tile), jnp.clip(2 * m - 1, 0, last), cos_a_ref, sin_a_ref,
         proj_a, proj_b, ymix_b, ymix_a, xa_a, dt_a, q_a, k_a)
    half(slice(tile, 2 * tile), jnp.clip(2 * m, 0, last), cos_b_ref, sin_b_ref,
         proj_b, proj_a, ymix_a, ymix_b, xa_b, dt_b, q_b, k_b)


def _mixer(x, cos_t, sin_t, w_in_p, conv_w, conv_b, dtb, alog, dskip, normw, gnw, gnb, w_out, ln_g, ln_b, tile):
    bsz, seqlen, _ = x.shape
    nt = seqlen // tile
    assert nt % 2 == 0
    ntiles = bsz * nt
    npairs = ntiles // 2
    const = lambda m: (0, 0)

    def cspec(shape):
        return pl.BlockSpec(shape, const, pipeline_mode=pl.Buffered(1))

    def pair_index(i):
        return (i // (nt // 2), i % (nt // 2), 0)

    def p_index(m):
        return pair_index(jnp.minimum(m, npairs - 1))

    def o_index(m):
        return pair_index(jnp.maximum(m - 1, 0))

    def rope_a_index(m):
        return (jnp.clip(2 * m - 1, 0, ntiles - 1) % nt, 0)

    def rope_b_index(m):
        return (jnp.clip(2 * m, 0, ntiles - 1) % nt, 0)

    per_half = [
        pltpu.VMEM((tile + SUBLANES, IN_PACKED), F32),
        pltpu.VMEM((tile, MIX_WIDTH), BF16),
        pltpu.VMEM((tile, SSD_CONV_DIM), F32),
        pltpu.VMEM((tile, LANES), F32),
        pltpu.VMEM((tile, RET_QK), F32),
        pltpu.VMEM((tile, RET_QK), F32),
    ]
    return pl.pallas_call(
        functools.partial(_mixer_kernel, tile=tile, nt=nt, ntiles=ntiles),
        grid=(npairs + 1,),
        in_specs=[
            pl.BlockSpec((1, 2 * tile, D_MODEL), p_index),
            pl.BlockSpec((1, 2 * tile, D_MODEL), o_index),
            pl.BlockSpec((tile, LANES), rope_a_index),
            pl.BlockSpec((tile, LANES), rope_a_index),
            pl.BlockSpec((tile, LANES), rope_b_index),
            pl.BlockSpec((tile, LANES), rope_b_index),
            cspec((D_MODEL, IN_PACKED)),
            cspec((SSD_CONV, SSD_CONV_DIM)),
            cspec((1, SSD_CONV_DIM)),
            cspec((1, LANES)),
            cspec((1, LANES)),
            cspec((1, SSD_INNER)),
            cspec((1, SSD_INNER)),
            cspec((1, RET_INNER)),
            cspec((1, RET_INNER)),
            cspec((MIX_WIDTH, D_MODEL)),
            cspec((1, D_MODEL)),
            cspec((1, D_MODEL)),
        ],
        out_specs=pl.BlockSpec((1, 2 * tile, D_MODEL), o_index),
        out_shape=jax.ShapeDtypeStruct((bsz, seqlen, D_MODEL), F32),
        scratch_shapes=[s for pair in zip(per_half, per_half) for s in pair] + [
            pltpu.VMEM((SUBLANES, SSD_CONV_DIM), F32),
            pltpu.VMEM((SSD_GROUPS, SSD_STATE, GROUP_WIDTH), F32),
            pltpu.VMEM((RET_HEADS, LANES, RET_V_DIM), F32),
            pltpu.VMEM((RET_HEADS, CHUNK, CHUNK), F32),
            pltpu.VMEM((RET_HEADS, CHUNK, LANES), F32),
            pltpu.VMEM((RET_HEADS, CHUNK, LANES), F32),
            pltpu.VMEM((LANES, 2 * SSD_INNER), BF16),
            pltpu.VMEM((CHUNK, CHUNK), BF16),
        ],
        compiler_params=pltpu.CompilerParams(
            dimension_semantics=("arbitrary",), vmem_limit_bytes=MIXER_VMEM_LIMIT_BYTES),
        name="mixer",
    )(x, x, cos_t, sin_t, cos_t, sin_t, w_in_p, conv_w, conv_b, dtb, alog, dskip, normw, gnw, gnb, w_out, ln_g, ln_b)


def _xattn_kernel(x_ref, kt_ref, v_ref, wq_ref, wo_ref, lng_ref, lnb_ref, out_ref, o_s):
    x = x_ref[0]
    q = _dot(x.astype(BF16), wq_ref[...]).astype(BF16)
    for h in range(XATTN_HEADS):
        hs = slice(h * XATTN_HEAD_DIM, (h + 1) * XATTN_HEAD_DIM)
        s = _dot(q[:, hs], kt_ref[0, hs, :]) * (XATTN_HEAD_DIM ** -0.5)
        m = jnp.max(s, axis=-1, keepdims=True)
        p = jnp.exp(s - m)
        p = p / jnp.sum(p, axis=-1, keepdims=True)
        o_s[:, hs] = _dot(p.astype(BF16), v_ref[0, :, hs]).astype(BF16)
    xa = _dot(o_s[...], wo_ref[...])
    out_ref[0] = _layer_norm(ALPHA * x + xa, lng_ref[...], lnb_ref[...])


def _xattn(x, kt, vm, wq, wo, ln_g, ln_b, tile):
    bsz, seqlen, _ = x.shape
    mlen = vm.shape[1]
    const = lambda b, t: (0, 0)

    def cspec(shape):
        return pl.BlockSpec(shape, const, pipeline_mode=pl.Buffered(1))

    return pl.pallas_call(
        _xattn_kernel,
        grid=(bsz, seqlen // tile),
        in_specs=[
            pl.BlockSpec((1, tile, D_MODEL), lambda b, t: (b, t, 0)),
            pl.BlockSpec((1, D_MODEL, mlen), lambda b, t: (b, 0, 0)),
            pl.BlockSpec((1, mlen, D_MODEL), lambda b, t: (b, 0, 0)),
            cspec((D_MODEL, D_MODEL)),
            cspec((D_MODEL, D_MODEL)),
            cspec((1, D_MODEL)),
            cspec((1, D_MODEL)),
        ],
        out_specs=pl.BlockSpec((1, tile, D_MODEL), lambda b, t: (b, t, 0)),
        out_shape=jax.ShapeDtypeStruct((bsz, seqlen, D_MODEL), F32),
        scratch_shapes=[pltpu.VMEM((tile, D_MODEL), BF16)],
        compiler_params=pltpu.CompilerParams(
            dimension_semantics=("arbitrary", "arbitrary"), vmem_limit_bytes=VMEM_LIMIT_BYTES),
        name="xattn",
    )(x, kt, vm, wq, wo, ln_g, ln_b)


def _ffn_kernel(x_ref, wup_ref, bup_ref, convw_ref, convb_ref, wdown_ref, lng_ref, lnb_ref,
                out_ref, acc_s, *ext, tile):
    t = pl.program_id(1)
    tail = slice(tile, tile + SUBLANES)
    nblk = D_FF_PAD // FFN_BLOCK

    @pl.when(t == 0)
    def _reset_halo():
        for e in ext:
            e[tail, :] = jnp.zeros((SUBLANES, FFN_BLOCK), F32)

    x = x_ref[0]
    xb = x.astype(BF16)

    def up(i):
        cs = slice(i * FFN_BLOCK, (i + 1) * FFN_BLOCK)
        ext[i][0:SUBLANES, :] = ext[i][tail, :]
        ext[i][SUBLANES:SUBLANES + tile, :] = _dot(xb, wup_ref[:, cs]) + bup_ref[:, cs]

    def conv(i):
        cs = slice(i * FFN_BLOCK, (i + 1) * FFN_BLOCK)
        y = convb_ref[:, cs]
        for k in range(FFN_CONV):
            off = SUBLANES - (FFN_CONV - 1) + k
            y = y + convw_ref[k:k + 1, cs] * ext[i][off:off + tile, :]
        return y

    def down(j, gated):
        part = _dot(gated, wdown_ref[j * FFN_BLOCK:(j + 1) * FFN_BLOCK, :])
        if j == 0:
            acc_s[...] = part
        else:
            acc_s[...] += part

    ahead = 2
    for i in range(ahead):
        up(i)
        up(nblk + i)
    gated_prev = None
    for j in range(nblk):
        if j + ahead < nblk:
            up(j + ahead)
            up(nblk + j + ahead)
        if gated_prev is not None:
            down(j - 1, gated_prev)
        gated_prev = (_silu(conv(j)) * conv(nblk + j)).astype(BF16)
    down(nblk - 1, gated_prev)
    out_ref[0] = _layer_norm(ALPHA * x + acc_s[...], lng_ref[...], lnb_ref[...])


def _ffn(x, w_up, b_up, conv_w, conv_b, w_down, ln_g, ln_b, tile):
    bsz, seqlen, _ = x.shape
    const = lambda b, t: (0, 0)

    def cspec(shape):
        return pl.BlockSpec(shape, const, pipeline_mode=pl.Buffered(1))

    return pl.pallas_call(
        functools.partial(_ffn_kernel, tile=tile),
        grid=(bsz, seqlen // tile),
        in_specs=[
            pl.BlockSpec((1, tile, D_MODEL), lambda b, t: (b, t, 0)),
            cspec((D_MODEL, 2 * D_FF_PAD)),
            cspec((1, 2 * D_FF_PAD)),
            cspec((FFN_CONV, 2 * D_FF_PAD)),
            cspec((1, 2 * D_FF_PAD)),
            cspec((D_FF_PAD, D_MODEL)),
            cspec((1, D_MODEL)),
            cspec((1, D_MODEL)),
        ],
        out_specs=pl.BlockSpec((1, tile, D_MODEL), lambda b, t: (b, t, 0)),
        out_shape=jax.ShapeDtypeStruct((bsz, seqlen, D_MODEL), F32),
        scratch_shapes=[pltpu.VMEM((tile, D_MODEL), F32)] + [
            pltpu.VMEM((tile + SUBLANES, FFN_BLOCK), F32) for _ in range(2 * D_FF_PAD // FFN_BLOCK)],
        compiler_params=pltpu.CompilerParams(
            dimension_semantics=("arbitrary", "arbitrary"), vmem_limit_bytes=VMEM_LIMIT_BYTES),
        name="ffn",
    )(x, w_up, b_up, conv_w, conv_b, w_down, ln_g, ln_b)


def _qk_perm():
    half = RET_QK_DIM // 2
    idx = []
    for pr in range(RET_HEADS // 2):
        for part in range(2):
            for j in range(2):
                h = 2 * pr + j
                idx.extend(range(h * RET_QK_DIM + part * half, h * RET_QK_DIM + (part + 1) * half))
    return np.asarray(idx, dtype=np.int32)


def _pad_last(a, n):
    return jnp.pad(a, [(0, 0)] * (a.ndim - 1) + [(0, n - a.shape[-1])])


def _pack_in_proj(w_in):
    o = np.cumsum((0, SSD_INNER, SSD_CONV_DIM, SSD_HEADS, RET_QK, RET_QK, RET_INNER, RET_INNER))
    z, xbc, dt, q, k, v, g = [w_in[:, o[i]:o[i + 1]] for i in range(7)]
    perm = _qk_perm()
    return jnp.concatenate([z, xbc, _pad_last(dt, LANES), q[:, perm], k[:, perm], v, g], axis=-1).astype(BF16)


def _pack_ffn_cols(a):
    return jnp.concatenate([_pad_last(a[..., :D_FF], D_FF_PAD), _pad_last(a[..., D_FF:], D_FF_PAD)], axis=-1)


def _rope_tables(seqlen):
    pos = jnp.arange(seqlen, dtype=F32)
    freqs = 1.0 / (ROPE_BASE ** jnp.linspace(0.0, 1.0, RET_QK_DIM // 2, dtype=F32))
    ang = pos[:, None] * freqs[None, :]
    cos, sin = jnp.cos(ang), jnp.sin(ang)
    return jnp.concatenate([cos, cos, cos, cos], axis=-1), jnp.concatenate([-sin, -sin, sin, sin], axis=-1)


def _pick_tile(seqlen, want):
    tile = min(want, seqlen)
    assert seqlen % tile == 0 and tile % CHUNK == 0
    return tile


def kernel(x, mem, w_in, ssd_conv_w, ssd_conv_b, ssd_dt_bias, ssd_a_log, ssd_d, ssd_norm_w, ret_gn_w, ret_gn_b, w_mix_out, ln1_g, ln1_b, w_xq, w_xk, w_xv, w_xo, ln2_g, ln2_b, w_ffn_up, b_ffn_up, ffn_conv_w, ffn_conv_b, w_ffn_down, ln3_g, ln3_b):
    seqlen = x.shape[1]
    tile = _pick_tile(seqlen, MIXER_TILE)
    xattn_tile = _pick_tile(seqlen, XATTN_TILE)
    ffn_tile = _pick_tile(seqlen, FFN_TILE)
    cos_t, sin_t = _rope_tables(seqlen)
    row = lambda a: a.reshape(1, -1)
    for l in range(DEPTH):
        kt, vm = _memkv(mem, w_xk[l].astype(BF16), w_xv[l].astype(BF16))
        x = _mixer(
            x, cos_t, sin_t, _pack_in_proj(w_in[l]), ssd_conv_w[l], row(ssd_conv_b[l]),
            _pad_last(row(ssd_dt_bias[l]), LANES), _pad_last(row(ssd_a_log[l]), LANES),
            row(jnp.repeat(ssd_d[l], SSD_HEAD_DIM)), row(ssd_norm_w[l]), row(ret_gn_w[l]), row(ret_gn_b[l]),
            w_mix_out[l].astype(BF16), row(ln1_g[l]), row(ln1_b[l]), tile)
        x = _xattn(x, kt, vm, w_xq[l].astype(BF16), w_xo[l].astype(BF16), row(ln2_g[l]), row(ln2_b[l]),
                   xattn_tile)
        x = _ffn(
            x, _pack_ffn_cols(w_ffn_up[l]).astype(BF16), row(_pack_ffn_cols(b_ffn_up[l])),
            _pack_ffn_cols(ffn_conv_w[l]), row(_pack_ffn_cols(ffn_conv_b[l])),
            _pad_last(w_ffn_down[l].T, D_FF_PAD).T.astype(BF16), row(ln3_g[l]), row(ln3_b[l]), ffn_tile)
    return x
```

```python
import functools

import numpy as np
import jax
import jax.numpy as jnp
from jax import lax
from jax.experimental import pallas as pl
from jax.experimental.pallas import tpu as pltpu

F32 = jnp.float32
BF16 = jnp.bfloat16

D_MODEL = 1024
DEPTH = 1
SSD_HEADS = 16
SSD_HEAD_DIM = 64
SSD_INNER = SSD_HEADS * SSD_HEAD_DIM
SSD_GROUPS = 2
SSD_STATE = 128
SSD_CONV = 4
SSD_CONV_DIM = SSD_INNER + 2 * SSD_GROUPS * SSD_STATE
GROUP_WIDTH = SSD_INNER // SSD_GROUPS
RET_HEADS = 8
RET_QK_DIM = 64
RET_V_DIM = 128
RET_QK = RET_HEADS * RET_QK_DIM
RET_INNER = RET_HEADS * RET_V_DIM
CHUNK = 128
ROPE_BASE = 10000.0
MIX_WIDTH = SSD_INNER + RET_INNER
XATTN_HEADS = 4
XATTN_HEAD_DIM = D_MODEL // XATTN_HEADS
D_FF = 2752
FFN_CONV = 3
ALPHA = (2.0 * DEPTH) ** 0.25
EPS = 1e-5

LANES = 128
SUBLANES = 8
VMEM_LIMIT_BYTES = 56 * 1024 * 1024
MIXER_VMEM_LIMIT_BYTES = 60 * 1024 * 1024

Z0 = 0
XBC0 = Z0 + SSD_INNER
DT0 = XBC0 + SSD_CONV_DIM
Q0 = DT0 + LANES
K0 = Q0 + RET_QK
V0 = K0 + RET_QK
G0 = V0 + RET_INNER
IN_PACKED = G0 + RET_INNER

D_FF_PAD = ((D_FF + LANES - 1) // LANES) * LANES
FFN_BLOCK = 256
PROJ_BLOCK = 256
OUT_BLOCK = 256
assert D_FF_PAD % FFN_BLOCK == 0 and DT0 % PROJ_BLOCK == 0 and (IN_PACKED - Q0) % PROJ_BLOCK == 0

MIXER_TILE = 256
XATTN_TILE = 1024
XATTN_STREAM_ROWS = 256
FFN_TILE = 256
FFN_SUB_ROWS = 256


def _dot(a, b):
    return jnp.dot(a, b, preferred_element_type=F32)


def _dot_nt(a, b):
    return lax.dot_general(a, b, (((1,), (1,)), ((), ())), preferred_element_type=F32)


def _dot_tn(a, b):
    return lax.dot_general(a, b, (((0,), (0,)), ((), ())), preferred_element_type=F32)


def _silu(v):
    return v / (1.0 + jnp.exp(-v))


def _layer_norm(r, g, b):
    mu = jnp.mean(r, axis=-1, keepdims=True)
    rc = r - mu
    var = jnp.mean(rc * rc, axis=-1, keepdims=True)
    return rc * lax.rsqrt(var + EPS) * g + b


def _split2(w):
    hi = w.astype(BF16)
    lo = (w - hi.astype(F32)).astype(BF16)
    return hi, lo


def _split3(w):
    hi = w.astype(BF16)
    r1 = w - hi.astype(F32)
    mid = r1.astype(BF16)
    lo = (r1 - mid.astype(F32)).astype(BF16)
    return hi, mid, lo


def _ret_log_gamma():
    return np.log1p(-np.exp2(-5.0 - np.arange(RET_HEADS, dtype=np.float32))).astype(np.float32)


def _pair_member(lane_idx):
    half = RET_QK_DIM // 2
    return jnp.bitwise_and(jnp.right_shift(lane_idx, half.bit_length() - 1), 1)


def _memkv_kernel(mem_ref, wk_ref, wv_ref, kt_ref, v_ref):
    m = mem_ref[0].astype(BF16)
    kt_ref[0] = lax.dot_general(wk_ref[...], m, (((0,), (1,)), ((), ())),
                                preferred_element_type=F32).astype(BF16)
    v_ref[0] = _dot(m, wv_ref[...]).astype(BF16)


def _memkv(mem, wk, wv):
    bsz, mlen, _ = mem.shape
    const = lambda b: (0, 0)
    return pl.pallas_call(
        _memkv_kernel,
        grid=(bsz,),
        in_specs=[
            pl.BlockSpec((1, mlen, D_MODEL), lambda b: (b, 0, 0)),
            pl.BlockSpec((D_MODEL, D_MODEL), const),
            pl.BlockSpec((D_MODEL, D_MODEL), const),
        ],
        out_specs=[
            pl.BlockSpec((1, D_MODEL, mlen), lambda b: (b, 0, 0)),
            pl.BlockSpec((1, mlen, D_MODEL), lambda b: (b, 0, 0)),
        ],
        out_shape=[
            jax.ShapeDtypeStruct((bsz, D_MODEL, mlen), BF16),
            jax.ShapeDtypeStruct((bsz, mlen, D_MODEL), BF16),
        ],
        compiler_params=pltpu.CompilerParams(
            dimension_semantics=("arbitrary",), vmem_limit_bytes=VMEM_LIMIT_BYTES),
        name="memkv",
    )(mem, wk, wv)


def _proj_blocks():
    blocks = [(c, PROJ_BLOCK) for c in range(0, DT0, PROJ_BLOCK)]
    blocks.append((DT0, LANES))
    blocks += [(c, PROJ_BLOCK) for c in range(Q0, IN_PACKED, PROJ_BLOCK)]
    return blocks


def _mixer_kernel(xp_ref, xr_ref, cos_a_ref, sin_a_ref, cos_b_ref, sin_b_ref,
                  win_ref, convw_ref, convb_ref, dtb_ref, alog_ref,
                  dskip_ref, normw_ref, gnw_ref, gnb_ref, wout_ref, lng_ref, lnb_ref,
                  out_ref,
                  proj_a, proj_b, ymix_a, ymix_b, xa_a, xa_b, dt_a, dt_b, q_a, q_b, k_a, k_b,
                  halo_s, hs_s, r_s, dec_s, qd_s, kd_s, exp_s, tri_s, *, tile, nt, ntiles):
    m = pl.program_id(0)
    log_gamma = _ret_log_gamma()
    top = SUBLANES

    @pl.when(m == 0)
    def _init():
        proj_b[...] = jnp.zeros_like(proj_b)
        ymix_a[...] = jnp.zeros_like(ymix_a)
        hs_s[...] = jnp.zeros_like(hs_s)
        r_s[...] = jnp.zeros_like(r_s)
        halo_s[...] = jnp.zeros_like(halo_s)
        row = lax.broadcasted_iota(jnp.int32, (CHUNK, CHUNK), 0)
        col = lax.broadcasted_iota(jnp.int32, (CHUNK, CHUNK), 1)
        tri_s[...] = jnp.where(row >= col, 1.0, 0.0).astype(BF16)
        erow = lax.broadcasted_iota(jnp.int32, (LANES, 2 * SSD_INNER), 0)
        ecol = lax.broadcasted_iota(jnp.int32, (LANES, 2 * SSD_INNER), 1)
        slot = erow // SSD_HEADS
        lo = (slot // 2) * SSD_INNER + (erow % SSD_HEADS) * SSD_HEAD_DIM
        hit = jnp.logical_and(slot < 4, jnp.logical_and(ecol >= lo, ecol < lo + SSD_HEAD_DIM))
        exp_s[...] = jnp.where(hit, 1.0, 0.0).astype(BF16)
        dist = (row - col).astype(F32)
        pos = row.astype(F32)
        for h in range(RET_HEADS):
            lg = float(log_gamma[h])
            headmask = jnp.where(_pair_member(col) == (h % 2), 1.0, 0.0)
            dec_s[h] = (RET_QK_DIM ** -0.5) * jnp.exp(jnp.where(dist >= 0, dist * lg, -jnp.inf))
            qd_s[h] = jnp.exp((pos + 1.0) * lg) * headmask
            kd_s[h] = (RET_QK_DIM ** -0.5) * jnp.exp((CHUNK - 1.0 - pos) * lg)

    lane = lax.broadcasted_iota(jnp.int32, (1, LANES), 1)
    a_head = jnp.where(lane < SSD_HEADS, -jnp.exp(alog_ref[...]), 0.0)
    row = lax.broadcasted_iota(jnp.int32, (CHUNK, CHUNK), 0)
    col = lax.broadcasted_iota(jnp.int32, (CHUNK, CHUNK), 1)
    causal = row >= col
    first_half = col < SSD_HEAD_DIM
    pair_masks = [jnp.where(_pair_member(col) == j, 1.0, 0.0) for j in range(2)]

    head_lanes = lane < SSD_HEADS

    def pack_heads(parts):
        out = None
        for i, part in enumerate(parts):
            part = jnp.where(head_lanes, part, 0.0)
            if i:
                part = pltpu.roll(part, SSD_HEADS * i, 1)
            out = part if out is None else out + part
        return out

    def unpack_heads_sum(packed, nparts):
        out = packed
        for i in range(1, nparts):
            out = out + pltpu.roll(packed, LANES - SSD_HEADS * i, 1)
        return out

    def split2_f32(w):
        hi, lo = _split2(w)
        return [hi.astype(F32), lo.astype(F32)]

    def half(x_rows, c_tile, cos_ref, sin_ref, pp, pc, ymix_c, ymix_o, xa_s, dt_s, q_s, k_s):
        fresh = lax.rem(c_tile, nt) == 0

        xb = xp_ref[0, x_rows, :].astype(BF16)

        def p_task(c0, width):
            def run():
                pp[top:top + tile, c0:c0 + width] = _dot(xb, win_ref[:, c0:c0 + width])
            return run

        p_tasks = [p_task(c0, width) for c0, width in _proj_blocks()]

        def o_task(j):
            def run():
                cs = slice(j * OUT_BLOCK, (j + 1) * OUT_BLOCK)
                out_ref[0, x_rows, cs] = ALPHA * xr_ref[0, x_rows, cs] + _dot(ymix_o[...], wout_ref[:, cs])
            return run

        def o_finish():
            out_ref[0, x_rows, :] = _layer_norm(out_ref[0, x_rows, :], lng_ref[...], lnb_ref[...])

        o_tasks = [o_task(j) for j in range(D_MODEL // OUT_BLOCK)]

        def conv_task(j):
            def run():
                cs = slice(j * PROJ_BLOCK, (j + 1) * PROJ_BLOCK)
                ps = slice(XBC0 + j * PROJ_BLOCK, XBC0 + (j + 1) * PROJ_BLOCK)
                pc[0:SUBLANES, ps] = jnp.where(fresh, 0.0, halo_s[:, cs])
                conv = convb_ref[:, cs]
                for k in range(SSD_CONV):
                    off = top - (SSD_CONV - 1) + k
                    conv = conv + convw_ref[k:k + 1, cs] * pc[off:off + tile, ps]
                xa_s[:, cs] = _silu(conv)
                halo_s[:, cs] = pc[tile:tile + SUBLANES, ps]
            return run

        def dt_task():
            dtr = pc[top:top + tile, DT0:DT0 + LANES] + dtb_ref[...]
            dt_s[...] = jnp.maximum(dtr, 0.0) + jnp.log1p(jnp.exp(-jnp.abs(dtr)))

        def rot_task(j):
            def run():
                cos = cos_ref[...]
                sin = sin_ref[...]
                qj = pc[top:top + tile, Q0 + j * LANES:Q0 + (j + 1) * LANES]
                kj = pc[top:top + tile, K0 + j * LANES:K0 + (j + 1) * LANES]
                sl = slice(j * LANES, (j + 1) * LANES)
                q_s[:, sl] = qj * cos + pltpu.roll(qj, LANES // 2, 1) * sin
                k_s[:, sl] = kj * cos + pltpu.roll(kj, LANES // 2, 1) * sin
            return run

        def chunk_tasks(c):
            rows = slice(c * CHUNK, (c + 1) * CHUNK)
            prow = slice(top + c * CHUNK, top + (c + 1) * CHUNK)
            shared = {}

            def state(ref_value):
                return jnp.where(fresh, 0.0, ref_value) if c == 0 else ref_value

            def pre():
                dt = dt_s[rows, :]
                a = dt * a_head
                a_parts = pack_heads([p.astype(F32) for p in _split3(a)]).astype(BF16)
                acs = unpack_heads_sum(_dot(tri_s[...], a_parts), 3)
                tot = acs[CHUNK - 1:CHUNK, :]
                shared["acs"] = acs
                w1 = dt * jnp.exp(tot - acs)
                dfs = jnp.exp(acs)
                both = _dot(pack_heads(split2_f32(w1) + split2_f32(dfs)).astype(BF16), exp_s[...])
                shared["w1x"] = both[:, 0:SSD_INNER]
                shared["dfsx"] = both[:, SSD_INNER:2 * SSD_INNER]
                cd = jnp.broadcast_to(jnp.exp(tot), (SUBLANES, LANES))
                shared["cdx"] = _dot(pack_heads(split2_f32(cd)).astype(BF16), exp_s[:, 0:SSD_INNER])[0:1, :]
                shared["acs_t"] = acs.T
                shared["dt_t"] = dt.T

            def ssd_tasks(g):
                gs = slice(g * GROUP_WIDTH, (g + 1) * GROUP_WIDTH)
                b0 = SSD_INNER + g * SSD_STATE
                c0 = SSD_INNER + SSD_GROUPS * SSD_STATE + g * SSD_STATE
                grp = {"us": []}

                def head():
                    bm = xa_s[rows, b0:b0 + SSD_STATE].astype(BF16)
                    cm = xa_s[rows, c0:c0 + SSD_STATE].astype(BF16)
                    grp["cb"] = _dot_nt(cm, bm)
                    h_prev = state(hs_s[g])
                    grp["y_off"] = _dot(cm, h_prev.astype(BF16)) * shared["dfsx"][:, gs]
                    states = _dot_tn(bm, (xa_s[rows, gs] * shared["w1x"][:, gs]).astype(BF16))
                    hs_s[g] = h_prev * shared["cdx"][:, gs] + states

                def pair(pr):
                    def run():
                        acs, acs_t, dt_t = shared["acs"], shared["acs_t"], shared["dt_t"]
                        att = []
                        for j in range(2):
                            h = g * (SSD_HEADS // SSD_GROUPS) + 2 * pr + j
                            seg = jnp.exp(jnp.where(causal, acs[:, h:h + 1] - acs_t[h:h + 1, :], -jnp.inf))
                            att.append((grp["cb"] * seg * dt_t[h:h + 1, :]).astype(BF16))
                        ps = slice(pr * LANES, (pr + 1) * LANES)
                        gl = slice(g * GROUP_WIDTH + pr * LANES, g * GROUP_WIDTH + (pr + 1) * LANES)
                        xp = xa_s[rows, gl]
                        rhs = jnp.concatenate([jnp.where(first_half, xp, 0.0).astype(BF16),
                                               jnp.where(first_half, 0.0, xp).astype(BF16)], axis=0)
                        y = _dot(jnp.concatenate(att, axis=1), rhs)
                        y = y + grp["y_off"][:, ps] + dskip_ref[:, gl] * xp
                        grp["us"].append(y * _silu(pc[prow, Z0 + gl.start:Z0 + gl.stop]))
                    return run

                def norm():
                    us = grp["us"]
                    ssq = us[0] * us[0]
                    for u in us[1:]:
                        ssq = ssq + u * u
                    scale = lax.rsqrt(jnp.sum(ssq, axis=-1, keepdims=True) * (1.0 / GROUP_WIDTH) + EPS)
                    for pr, u in enumerate(us):
                        gl = slice(g * GROUP_WIDTH + pr * LANES, g * GROUP_WIDTH + (pr + 1) * LANES)
                        ymix_c[rows, gl] = (u * scale * normw_ref[:, gl]).astype(BF16)

                return [head] + [pair(pr) for pr in range(GROUP_WIDTH // LANES)] + [norm]

            def ret_scores(pr):
                def run():
                    ps = slice(pr * LANES, (pr + 1) * LANES)
                    qp = q_s[rows, ps]
                    kp = k_s[rows, ps]
                    kcat = jnp.concatenate([(kp * pair_masks[j]).astype(BF16) for j in range(2)], axis=0)
                    pair[pr] = {"s2": _dot_nt(qp.astype(BF16), kcat)}
                    vk = []
                    for j in range(2):
                        h = 2 * pr + j
                        vk.append((pc[prow, V0 + h * RET_V_DIM:V0 + (h + 1) * RET_V_DIM] * kd_s[h]).astype(BF16))
                    pair[pr]["st2"] = _dot_tn(kp.astype(BF16), jnp.concatenate(vk, axis=1))
                return run

            def ret_task(h):
                def run():
                    pr, j = divmod(h, 2)
                    ps = slice(pr * LANES, (pr + 1) * LANES)
                    js = slice(j * RET_V_DIM, (j + 1) * RET_V_DIM)
                    qp = q_s[rows, ps]
                    hl = slice(h * RET_V_DIM, (h + 1) * RET_V_DIM)
                    s = pair[pr]["s2"][:, js] * dec_s[h]
                    qd = (qp * qd_s[h]).astype(BF16)
                    vh = pc[prow, V0 + hl.start:V0 + hl.stop].astype(BF16)
                    r_prev = state(r_s[h])
                    y = _dot(jnp.concatenate([s.astype(BF16), qd], axis=1),
                             jnp.concatenate([vh, r_prev.astype(BF16)], axis=0))
                    r_s[h] = r_prev * float(np.exp(np.float32(CHUNK) * log_gamma[h])) + pair[pr]["st2"][:, js]
                    mu = jnp.mean(y, axis=-1, keepdims=True)
                    yc = y - mu
                    var = jnp.mean(yc * yc, axis=-1, keepdims=True)
                    yn = yc * lax.rsqrt(var + EPS) * gnw_ref[:, hl] + gnb_ref[:, hl]
                    gate = _silu(pc[prow, G0 + hl.start:G0 + hl.stop])
                    ymix_c[rows, SSD_INNER + hl.start:SSD_INNER + hl.stop] = (gate * yn).astype(BF16)
                return run

            pair = {}
            tasks = [pre]
            for g in range(SSD_GROUPS):
                tasks += ssd_tasks(g)
            for pr in range(RET_HEADS // 2):
                tasks += [ret_scores(pr), ret_task(2 * pr), ret_task(2 * pr + 1)]
            return tasks

        c_tasks = [conv_task(j) for j in range(SSD_CONV_DIM // PROJ_BLOCK)] + [dt_task]
        c_tasks += [rot_task(j) for j in range(RET_QK // LANES)]
        for c in range(tile // CHUNK):
            c_tasks += chunk_tasks(c)

        m_tasks = p_tasks + o_tasks
        done = 0
        for i, task in enumerate(c_tasks):
            task()
            due = ((i + 1) * len(m_tasks)) // len(c_tasks)
            for mt in m_tasks[done:due]:
                mt()
            done = due
        o_finish()

    last = ntiles - 1
    half(slice(0, tile), jnp.clip(2 * m - 1, 0, last), cos_a_ref, sin_a_ref,
         proj_a, proj_b, ymix_b, ymix_a, xa_a, dt_a, q_a, k_a)
    half(slice(tile, 2 * tile), jnp.clip(2 * m, 0, last), cos_b_ref, sin_b_ref,
         proj_b, proj_a, ymix_a, ymix_b, xa_b, dt_b, q_b, k_b)


def _mixer(x, cos_t, sin_t, w_in_p, conv_w, conv_b, dtb, alog, dskip, normw, gnw, gnb, w_out, ln_g, ln_b, tile):
    bsz, seqlen, _ = x.shape
    nt = seqlen // tile
    assert nt % 2 == 0
    ntiles = bsz * nt
    npairs = ntiles // 2
    const = lambda m: (0, 0)

    def cspec(shape):
        return pl.BlockSpec(shape, const, pipeline_mode=pl.Buffered(1))

    def pair_index(i):
        return (i // (nt // 2), i % (nt // 2), 0)

    def p_index(m):
        return pair_index(jnp.minimum(m, npairs - 1))

    def o_index(m):
        return pair_index(jnp.maximum(m - 1, 0))

    def rope_a_index(m):
        return (jnp.clip(2 * m - 1, 0, ntiles - 1) % nt, 0)

    def rope_b_index(m):
        return (jnp.clip(2 * m, 0, ntiles - 1) % nt, 0)

    per_half = [
        pltpu.VMEM((tile + SUBLANES, IN_PACKED), F32),
        pltpu.VMEM((tile, MIX_WIDTH), BF16),
        pltpu.VMEM((tile, SSD_CONV_DIM), F32),
        pltpu.VMEM((tile, LANES), F32),
        pltpu.VMEM((tile, RET_QK), F32),
        pltpu.VMEM((tile, RET_QK), F32),
    ]
    return pl.pallas_call(
        functools.partial(_mixer_kernel, tile=tile, nt=nt, ntiles=ntiles),
        grid=(npairs + 1,),
        in_specs=[
            pl.BlockSpec((1, 2 * tile, D_MODEL), p_index),
            pl.BlockSpec((1, 2 * tile, D_MODEL), o_index),
            pl.BlockSpec((tile, LANES), rope_a_index),
            pl.BlockSpec((tile, LANES), rope_a_index),
            pl.BlockSpec((tile, LANES), rope_b_index),
            pl.BlockSpec((tile, LANES), rope_b_index),
            cspec((D_MODEL, IN_PACKED)),
            cspec((SSD_CONV, SSD_CONV_DIM)),
            cspec((1, SSD_CONV_DIM)),
            cspec((1, LANES)),
            cspec((1, LANES)),
            cspec((1, SSD_INNER)),
            cspec((1, SSD_INNER)),
            cspec((1, RET_INNER)),
            cspec((1, RET_INNER)),
            cspec((MIX_WIDTH, D_MODEL)),
            cspec((1, D_MODEL)),
            cspec((1, D_MODEL)),
        ],
        out_specs=pl.BlockSpec((1, 2 * tile, D_MODEL), o_index),
        out_shape=jax.ShapeDtypeStruct((bsz, seqlen, D_MODEL), F32),
        scratch_shapes=[s for pair in zip(per_half, per_half) for s in pair] + [
            pltpu.VMEM((SUBLANES, SSD_CONV_DIM), F32),
            pltpu.VMEM((SSD_GROUPS, SSD_STATE, GROUP_WIDTH), F32),
            pltpu.VMEM((RET_HEADS, LANES, RET_V_DIM), F32),
            pltpu.VMEM((RET_HEADS, CHUNK, CHUNK), F32),
            pltpu.VMEM((RET_HEADS, CHUNK, LANES), F32),
            pltpu.VMEM((RET_HEADS, CHUNK, LANES), F32),
            pltpu.VMEM((LANES, 2 * SSD_INNER), BF16),
            pltpu.VMEM((CHUNK, CHUNK), BF16),
        ],
        compiler_params=pltpu.CompilerParams(
            dimension_semantics=("arbitrary",), vmem_limit_bytes=MIXER_VMEM_LIMIT_BYTES),
        name="mixer",
    )(x, x, cos_t, sin_t, cos_t, sin_t, w_in_p, conv_w, conv_b, dtb, alog, dskip, normw, gnw, gnb, w_out, ln_g, ln_b)


def _xattn_kernel(x_ref, kt_ref, v_ref, wq_ref, wo_ref, lng_ref, lnb_ref, out_ref, *scratch, tile):
    nstream = tile // XATTN_STREAM_ROWS
    q_s, o_s = scratch[:nstream], scratch[nstream:]

    def stream(i):
        rows = slice(i * XATTN_STREAM_ROWS, (i + 1) * XATTN_STREAM_ROWS)

        def q_task(h):
            def run():
                hs = slice(h * XATTN_HEAD_DIM, (h + 1) * XATTN_HEAD_DIM)
                q_s[i][:, hs] = _dot(x_ref[0, rows, :].astype(BF16), wq_ref[:, hs]).astype(BF16)
            return run

        def head_task(h):
            def run():
                hs = slice(h * XATTN_HEAD_DIM, (h + 1) * XATTN_HEAD_DIM)
                s = _dot(q_s[i][:, hs], kt_ref[0, hs, :]) * (XATTN_HEAD_DIM ** -0.5)
                m = jnp.max(s, axis=-1, keepdims=True)
                p = jnp.exp(s - m)
                p = p / jnp.sum(p, axis=-1, keepdims=True)
                o_s[i][:, hs] = _dot(p.astype(BF16), v_ref[0, :, hs]).astype(BF16)
            return run

        def out_task():
            xa = _dot(o_s[i][...], wo_ref[...])
            out_ref[0, rows, :] = _layer_norm(ALPHA * x_ref[0, rows, :] + xa, lng_ref[...], lnb_ref[...])

        tasks = [q_task(0)]
        for h in range(XATTN_HEADS):
            if h + 1 < XATTN_HEADS:
                tasks.append(q_task(h + 1))
            tasks.append(head_task(h))
        return tasks + [out_task]

    streams = [stream(i) for i in range(nstream)]
    for group in zip(*streams):
        for task in group:
            task()


def _xattn(x, kt, vm, wq, wo, ln_g, ln_b, tile):
    bsz, seqlen, _ = x.shape
    mlen = vm.shape[1]
    const = lambda b, t: (0, 0)

    def cspec(shape):
        return pl.BlockSpec(shape, const, pipeline_mode=pl.Buffered(1))

    assert tile % XATTN_STREAM_ROWS == 0
    return pl.pallas_call(
        functools.partial(_xattn_kernel, tile=tile),
        grid=(bsz, seqlen // tile),
        in_specs=[
            pl.BlockSpec((1, tile, D_MODEL), lambda b, t: (b, t, 0)),
            pl.BlockSpec((1, D_MODEL, mlen), lambda b, t: (b, 0, 0)),
            pl.BlockSpec((1, mlen, D_MODEL), lambda b, t: (b, 0, 0)),
            cspec((D_MODEL, D_MODEL)),
            cspec((D_MODEL, D_MODEL)),
            cspec((1, D_MODEL)),
            cspec((1, D_MODEL)),
        ],
        out_specs=pl.BlockSpec((1, tile, D_MODEL), lambda b, t: (b, t, 0)),
        out_shape=jax.ShapeDtypeStruct((bsz, seqlen, D_MODEL), F32),
        scratch_shapes=[pltpu.VMEM((XATTN_STREAM_ROWS, D_MODEL), BF16)
                        for _ in range(2 * (tile // XATTN_STREAM_ROWS))],
        compiler_params=pltpu.CompilerParams(
            dimension_semantics=("arbitrary", "arbitrary"), vmem_limit_bytes=VMEM_LIMIT_BYTES),
        name="xattn",
    )(x, kt, vm, wq, wo, ln_g, ln_b)


def _ffn_kernel(x_ref, wup_ref, bup_ref, convw_ref, convb_ref, wdown_ref, lng_ref, lnb_ref,
                out_ref, *scratch, tile):
    t = pl.program_id(1)
    sub = FFN_SUB_ROWS
    nsub = tile // sub
    nblk = D_FF_PAD // FFN_BLOCK
    acc = scratch[:nsub]
    ext = [scratch[nsub + s * 2 * nblk:nsub + (s + 1) * 2 * nblk] for s in range(nsub)]
    tail = slice(sub, sub + SUBLANES)

    @pl.when(t == 0)
    def _reset_halo():
        for e in ext[nsub - 1]:
            e[tail, :] = jnp.zeros((SUBLANES, FFN_BLOCK), F32)

    xb = [x_ref[0, s * sub:(s + 1) * sub, :].astype(BF16) for s in range(nsub)]

    def up(s, i):
        cs = slice(i * FFN_BLOCK, (i + 1) * FFN_BLOCK)
        ext[s][i][0:SUBLANES, :] = ext[(s - 1) % nsub][i][tail, :]
        ext[s][i][SUBLANES:SUBLANES + sub, :] = _dot(xb[s], wup_ref[:, cs]) + bup_ref[:, cs]

    def conv(s, i):
        cs = slice(i * FFN_BLOCK, (i + 1) * FFN_BLOCK)
        y = convb_ref[:, cs]
        for k in range(FFN_CONV):
            off = SUBLANES - (FFN_CONV - 1) + k
            y = y + convw_ref[k:k + 1, cs] * ext[s][i][off:off + sub, :]
        return y

    def down(s, j, gated):
        part = _dot(gated, wdown_ref[j * FFN_BLOCK:(j + 1) * FFN_BLOCK, :])
        if j == 0:
            acc[s][...] = part
        else:
            acc[s][...] += part
        if j == nblk - 1:
            rows = slice(s * sub, (s + 1) * sub)
            out_ref[0, rows, :] = _layer_norm(ALPHA * x_ref[0, rows, :] + acc[s][...], lng_ref[...], lnb_ref[...])

    items = [(s, j) for s in range(nsub) for j in range(nblk)]
    ahead = 2

    def up_both(item):
        s, j = item
        up(s, j)
        up(s, nblk + j)

    for item in items[:ahead]:
        up_both(item)
    pending = None
    for idx, (s, j) in enumerate(items):
        if idx + ahead < len(items):
            up_both(items[idx + ahead])
        if pending is not None:
            down(*pending)
        pending = (s, j, (_silu(conv(s, j)) * conv(s, nblk + j)).astype(BF16))
    down(*pending)


def _ffn(x, w_up, b_up, conv_w, conv_b, w_down, ln_g, ln_b, tile):
    bsz, seqlen, _ = x.shape
    const = lambda b, t: (0, 0)

    def cspec(shape):
        return pl.BlockSpec(shape, const, pipeline_mode=pl.Buffered(1))

    return pl.pallas_call(
        functools.partial(_ffn_kernel, tile=tile),
        grid=(bsz, seqlen // tile),
        in_specs=[
            pl.BlockSpec((1, tile, D_MODEL), lambda b, t: (b, t, 0)),
            cspec((D_MODEL, 2 * D_FF_PAD)),
            cspec((1, 2 * D_FF_PAD)),
            cspec((FFN_CONV, 2 * D_FF_PAD)),
            cspec((1, 2 * D_FF_PAD)),
            cspec((D_FF_PAD, D_MODEL)),
            cspec((1, D_MODEL)),
            cspec((1, D_MODEL)),
        ],
        out_specs=pl.BlockSpec((1, tile, D_MODEL), lambda b, t: (b, t, 0)),
        out_shape=jax.ShapeDtypeStruct((bsz, seqlen, D_MODEL), F32),
        scratch_shapes=[pltpu.VMEM((FFN_SUB_ROWS, D_MODEL), F32) for _ in range(tile // FFN_SUB_ROWS)] + [
            pltpu.VMEM((FFN_SUB_ROWS + SUBLANES, FFN_BLOCK), F32)
            for _ in range((tile // FFN_SUB_ROWS) * 2 * D_FF_PAD // FFN_BLOCK)],
        compiler_params=pltpu.CompilerParams(
            dimension_semantics=("arbitrary", "arbitrary"), vmem_limit_bytes=VMEM_LIMIT_BYTES),
        name="ffn",
    )(x, w_up, b_up, conv_w, conv_b, w_down, ln_g, ln_b)


def _pair_layout(w):
    rows = w.shape[0]
    w = w.reshape(rows, RET_HEADS // 2, 2, 2, RET_QK_DIM // 2)
    return w.transpose(0, 1, 3, 2, 4).reshape(rows, RET_QK)


def _pad_last(a, n):
    return jnp.pad(a, [(0, 0)] * (a.ndim - 1) + [(0, n - a.shape[-1])])


def _pack_in_proj(w_in):
    o = np.cumsum((0, SSD_INNER, SSD_CONV_DIM, SSD_HEADS, RET_QK, RET_QK, RET_INNER, RET_INNER))
    z, xbc, dt, q, k, v, g = [w_in[:, o[i]:o[i + 1]] for i in range(7)]
    return jnp.concatenate(
        [z, xbc, _pad_last(dt, LANES), _pair_layout(q), _pair_layout(k), v, g], axis=-1).astype(BF16)


def _pack_ffn_cols(a):
    return jnp.concatenate([_pad_last(a[..., :D_FF], D_FF_PAD), _pad_last(a[..., D_FF:], D_FF_PAD)], axis=-1)


def _rope_tables(seqlen):
    pos = jnp.arange(seqlen, dtype=F32)
    freqs = 1.0 / (ROPE_BASE ** jnp.linspace(0.0, 1.0, RET_QK_DIM // 2, dtype=F32))
    ang = pos[:, None] * freqs[None, :]
    cos, sin = jnp.cos(ang), jnp.sin(ang)
    return jnp.concatenate([cos, cos, cos, cos], axis=-1), jnp.concatenate([-sin, -sin, sin, sin], axis=-1)


def _pick_tile(seqlen, want):
    tile = min(want, seqlen)
    assert seqlen % tile == 0 and tile % CHUNK == 0
    return tile


def kernel(x, mem, w_in, ssd_conv_w, ssd_conv_b, ssd_dt_bias, ssd_a_log, ssd_d, ssd_norm_w, ret_gn_w, ret_gn_b, w_mix_out, ln1_g, ln1_b, w_xq, w_xk, w_xv, w_xo, ln2_g, ln2_b, w_ffn_up, b_ffn_up, ffn_conv_w, ffn_conv_b, w_ffn_down, ln3_g, ln3_b):
    seqlen = x.shape[1]
    tile = _pick_tile(seqlen, MIXER_TILE)
    xattn_tile = _pick_tile(seqlen, XATTN_TILE)
    ffn_tile = _pick_tile(seqlen, FFN_TILE)
    cos_t, sin_t = _rope_tables(seqlen)
    row = lambda a: a.reshape(1, -1)
    for l in range(DEPTH):
        kt, vm = _memkv(mem, w_xk[l].astype(BF16), w_xv[l].astype(BF16))
        x = _mixer(
            x, cos_t, sin_t, _pack_in_proj(w_in[l]), ssd_conv_w[l], row(ssd_conv_b[l]),
            _pad_last(row(ssd_dt_bias[l]), LANES), _pad_last(row(ssd_a_log[l]), LANES),
            row(jnp.repeat(ssd_d[l], SSD_HEAD_DIM)), row(ssd_norm_w[l]), row(ret_gn_w[l]), row(ret_gn_b[l]),
            w_mix_out[l].astype(BF16), row(ln1_g[l]), row(ln1_b[l]), tile)
        x = _xattn(x, kt, vm, w_xq[l].astype(BF16), w_xo[l].astype(BF16), row(ln2_g[l]), row(ln2_b[l]),
                   xattn_tile)
        x = _ffn(
            x, _pack_ffn_cols(w_ffn_up[l]).astype(BF16), row(_pack_ffn_cols(b_ffn_up[l])),
            _pack_ffn_cols(ffn_conv_w[l]), row(_pack_ffn_cols(ffn_conv_b[l])),
            jnp.pad(w_ffn_down[l], ((0, D_FF_PAD - D_FF), (0, 0))).astype(BF16), row(ln3_g[l]), row(ln3_b[l]),
            ffn_tile)
    return x
```

```python
import functools

import numpy as np
import jax
import jax.numpy as jnp
from jax import lax
from jax.experimental import pallas as pl
from jax.experimental.pallas import tpu as pltpu

F32 = jnp.float32
BF16 = jnp.bfloat16

D_MODEL = 1024
DEPTH = 1
SSD_HEADS = 16
SSD_HEAD_DIM = 64
SSD_INNER = SSD_HEADS * SSD_HEAD_DIM
SSD_GROUPS = 2
SSD_STATE = 128
SSD_CONV = 4
SSD_CONV_DIM = SSD_INNER + 2 * SSD_GROUPS * SSD_STATE
GROUP_WIDTH = SSD_INNER // SSD_GROUPS
RET_HEADS = 8
RET_QK_DIM = 64
RET_V_DIM = 128
RET_QK = RET_HEADS * RET_QK_DIM
RET_INNER = RET_HEADS * RET_V_DIM
CHUNK = 128
ROPE_BASE = 10000.0
MIX_WIDTH = SSD_INNER + RET_INNER
XATTN_HEADS = 4
XATTN_HEAD_DIM = D_MODEL // XATTN_HEADS
D_FF = 2752
FFN_CONV = 3
ALPHA = (2.0 * DEPTH) ** 0.25
EPS = 1e-5

LANES = 128
SUBLANES = 8
VMEM_LIMIT_BYTES = 56 * 1024 * 1024
MIXER_VMEM_LIMIT_BYTES = 60 * 1024 * 1024

Z0 = 0
XBC0 = Z0 + SSD_INNER
DT0 = XBC0 + SSD_CONV_DIM
Q0 = DT0 + LANES
K0 = Q0 + RET_QK
V0 = K0 + RET_QK
G0 = V0 + RET_INNER
IN_PACKED = G0 + RET_INNER

D_FF_PAD = ((D_FF + LANES - 1) // LANES) * LANES
FFN_BLOCK = 256
PROJ_BLOCK = 256
OUT_BLOCK = 256
assert D_FF_PAD % FFN_BLOCK == 0 and DT0 % PROJ_BLOCK == 0 and (IN_PACKED - Q0) % PROJ_BLOCK == 0

MIXER_TILE = 256
XATTN_TILE = 2048
XATTN_STREAM_ROWS = 256
FFN_TILE = 256
FFN_SUB_ROWS = 256
FFN_AHEAD = 3
PREP_TASK_WEIGHT = 0.2


def _dot(a, b):
    return jnp.dot(a, b, preferred_element_type=F32)


def _dot_nt(a, b):
    return lax.dot_general(a, b, (((1,), (1,)), ((), ())), preferred_element_type=F32)


def _dot_tn(a, b):
    return lax.dot_general(a, b, (((0,), (0,)), ((), ())), preferred_element_type=F32)


def _silu(v):
    return v / (1.0 + jnp.exp(-v))


def _layer_norm(r, g, b):
    mu = jnp.mean(r, axis=-1, keepdims=True)
    rc = r - mu
    var = jnp.mean(rc * rc, axis=-1, keepdims=True)
    return rc * lax.rsqrt(var + EPS) * g + b


def _split2(w):
    hi = w.astype(BF16)
    lo = (w - hi.astype(F32)).astype(BF16)
    return hi, lo


def _split3(w):
    hi = w.astype(BF16)
    r1 = w - hi.astype(F32)
    mid = r1.astype(BF16)
    lo = (r1 - mid.astype(F32)).astype(BF16)
    return hi, mid, lo


def _ret_log_gamma():
    return np.log1p(-np.exp2(-5.0 - np.arange(RET_HEADS, dtype=np.float32))).astype(np.float32)


def _pair_member(lane_idx):
    half = RET_QK_DIM // 2
    return jnp.bitwise_and(jnp.right_shift(lane_idx, half.bit_length() - 1), 1)


def _memkv_kernel(mem_ref, wk_ref, wv_ref, kt_ref, v_ref):
    m = mem_ref[0].astype(BF16)
    kt_ref[0] = lax.dot_general(wk_ref[...], m, (((0,), (1,)), ((), ())),
                                preferred_element_type=F32).astype(BF16)
    v_ref[0] = _dot(m, wv_ref[...]).astype(BF16)


def _memkv(mem, wk, wv):
    bsz, mlen, _ = mem.shape
    const = lambda b: (0, 0)
    return pl.pallas_call(
        _memkv_kernel,
        grid=(bsz,),
        in_specs=[
            pl.BlockSpec((1, mlen, D_MODEL), lambda b: (b, 0, 0)),
            pl.BlockSpec((D_MODEL, D_MODEL), const),
            pl.BlockSpec((D_MODEL, D_MODEL), const),
        ],
        out_specs=[
            pl.BlockSpec((1, D_MODEL, mlen), lambda b: (b, 0, 0)),
            pl.BlockSpec((1, mlen, D_MODEL), lambda b: (b, 0, 0)),
        ],
        out_shape=[
            jax.ShapeDtypeStruct((bsz, D_MODEL, mlen), BF16),
            jax.ShapeDtypeStruct((bsz, mlen, D_MODEL), BF16),
        ],
        compiler_params=pltpu.CompilerParams(
            dimension_semantics=("arbitrary",), vmem_limit_bytes=VMEM_LIMIT_BYTES),
        name="memkv",
    )(mem, wk, wv)


def _proj_blocks():
    blocks = [(c, PROJ_BLOCK) for c in range(0, DT0, PROJ_BLOCK)]
    blocks.append((DT0, LANES))
    blocks += [(c, PROJ_BLOCK) for c in range(Q0, IN_PACKED, PROJ_BLOCK)]
    return blocks


def _mixer_kernel(xp_ref, xr_ref, cos_a_ref, sin_a_ref, cos_b_ref, sin_b_ref,
                  win_ref, convw_ref, convb_ref, dtb_ref, alog_ref,
                  dskip_ref, normw_ref, gnw_ref, gnb_ref, wout_ref, lng_ref, lnb_ref,
                  out_ref,
                  proj_a, proj_b, ymix_a, ymix_b, xa_a, xa_b, dt_a, dt_b, q_a, q_b, k_a, k_b,
                  halo_s, hs_s, r_s, dec_s, qd_s, kd_s, exp_s, tri_s, *, tile, nt, ntiles):
    m = pl.program_id(0)
    log_gamma = _ret_log_gamma()
    top = SUBLANES

    @pl.when(m == 0)
    def _init():
        proj_b[...] = jnp.zeros_like(proj_b)
        ymix_a[...] = jnp.zeros_like(ymix_a)
        hs_s[...] = jnp.zeros_like(hs_s)
        r_s[...] = jnp.zeros_like(r_s)
        halo_s[...] = jnp.zeros_like(halo_s)
        row = lax.broadcasted_iota(jnp.int32, (CHUNK, CHUNK), 0)
        col = lax.broadcasted_iota(jnp.int32, (CHUNK, CHUNK), 1)
        tri_s[...] = jnp.where(row >= col, 1.0, 0.0).astype(BF16)
        erow = lax.broadcasted_iota(jnp.int32, (LANES, 2 * SSD_INNER), 0)
        ecol = lax.broadcasted_iota(jnp.int32, (LANES, 2 * SSD_INNER), 1)
        slot = erow // SSD_HEADS
        lo = (slot // 2) * SSD_INNER + (erow % SSD_HEADS) * SSD_HEAD_DIM
        hit = jnp.logical_and(slot < 4, jnp.logical_and(ecol >= lo, ecol < lo + SSD_HEAD_DIM))
        exp_s[...] = jnp.where(hit, 1.0, 0.0).astype(BF16)
        dist = (row - col).astype(F32)
        pos = row.astype(F32)
        for h in range(RET_HEADS):
            lg = float(log_gamma[h])
            headmask = jnp.where(_pair_member(col) == (h % 2), 1.0, 0.0)
            dec_s[h] = (RET_QK_DIM ** -0.5) * jnp.exp(jnp.where(dist >= 0, dist * lg, -jnp.inf))
            qd_s[h] = jnp.exp((pos + 1.0) * lg) * headmask
            kd_s[h] = (RET_QK_DIM ** -0.5) * jnp.exp((CHUNK - 1.0 - pos) * lg)

    lane = lax.broadcasted_iota(jnp.int32, (1, LANES), 1)
    a_head = jnp.where(lane < SSD_HEADS, -jnp.exp(alog_ref[...]), 0.0)
    row = lax.broadcasted_iota(jnp.int32, (CHUNK, CHUNK), 0)
    col = lax.broadcasted_iota(jnp.int32, (CHUNK, CHUNK), 1)
    causal = row >= col
    first_half = col < SSD_HEAD_DIM
    pair_masks = [jnp.where(_pair_member(col) == j, 1.0, 0.0) for j in range(2)]

    head_lanes = lane < SSD_HEADS

    def pack_heads(parts):
        out = None
        for i, part in enumerate(parts):
            part = jnp.where(head_lanes, part, 0.0)
            if i:
                part = pltpu.roll(part, SSD_HEADS * i, 1)
            out = part if out is None else out + part
        return out

    def unpack_heads_sum(packed, nparts):
        out = packed
        for i in range(1, nparts):
            out = out + pltpu.roll(packed, LANES - SSD_HEADS * i, 1)
        return out

    def split2_f32(w):
        hi, lo = _split2(w)
        return [hi.astype(F32), lo.astype(F32)]

    def half(x_rows, c_tile, cos_ref, sin_ref, pp, pc, ymix_c, ymix_o, xa_s, dt_s, q_s, k_s):
        fresh = lax.rem(c_tile, nt) == 0

        xb = xp_ref[0, x_rows, :].astype(BF16)

        def p_task(c0, width):
            def run():
                pp[top:top + tile, c0:c0 + width] = _dot(xb, win_ref[:, c0:c0 + width])
            return run

        p_tasks = [p_task(c0, width) for c0, width in _proj_blocks()]

        def o_task(j):
            def run():
                cs = slice(j * OUT_BLOCK, (j + 1) * OUT_BLOCK)
                out_ref[0, x_rows, cs] = ALPHA * xr_ref[0, x_rows, cs] + _dot(ymix_o[...], wout_ref[:, cs])
            return run

        def o_finish():
            out_ref[0, x_rows, :] = _layer_norm(out_ref[0, x_rows, :], lng_ref[...], lnb_ref[...])

        o_tasks = [o_task(j) for j in range(D_MODEL // OUT_BLOCK)]

        def conv_task(j):
            def run():
                cs = slice(j * PROJ_BLOCK, (j + 1) * PROJ_BLOCK)
                ps = slice(XBC0 + j * PROJ_BLOCK, XBC0 + (j + 1) * PROJ_BLOCK)
                pc[0:SUBLANES, ps] = jnp.where(fresh, 0.0, halo_s[:, cs])
                conv = convb_ref[:, cs]
                for k in range(SSD_CONV):
                    off = top - (SSD_CONV - 1) + k
                    conv = conv + convw_ref[k:k + 1, cs] * pc[off:off + tile, ps]
                xa_s[:, cs] = _silu(conv)
                halo_s[:, cs] = pc[tile:tile + SUBLANES, ps]
            return run

        def dt_task():
            dtr = pc[top:top + tile, DT0:DT0 + LANES] + dtb_ref[...]
            dt_s[...] = jnp.maximum(dtr, 0.0) + jnp.log1p(jnp.exp(-jnp.abs(dtr)))

        def rot_task(j):
            def run():
                cos = cos_ref[...]
                sin = sin_ref[...]
                qj = pc[top:top + tile, Q0 + j * LANES:Q0 + (j + 1) * LANES]
                kj = pc[top:top + tile, K0 + j * LANES:K0 + (j + 1) * LANES]
                sl = slice(j * LANES, (j + 1) * LANES)
                q_s[:, sl] = qj * cos + pltpu.roll(qj, LANES // 2, 1) * sin
                k_s[:, sl] = kj * cos + pltpu.roll(kj, LANES // 2, 1) * sin
            return run

        def chunk_tasks(c):
            rows = slice(c * CHUNK, (c + 1) * CHUNK)
            prow = slice(top + c * CHUNK, top + (c + 1) * CHUNK)
            shared = {}

            def state(ref_value):
                return jnp.where(fresh, 0.0, ref_value) if c == 0 else ref_value

            def pre():
                dt = dt_s[rows, :]
                a = dt * a_head
                a_parts = pack_heads([p.astype(F32) for p in _split3(a)]).astype(BF16)
                acs = unpack_heads_sum(_dot(tri_s[...], a_parts), 3)
                tot = acs[CHUNK - 1:CHUNK, :]
                shared["acs"] = acs
                w1 = dt * jnp.exp(tot - acs)
                dfs = jnp.exp(acs)
                both = _dot(pack_heads(split2_f32(w1) + split2_f32(dfs)).astype(BF16), exp_s[...])
                shared["w1x"] = both[:, 0:SSD_INNER]
                shared["dfsx"] = both[:, SSD_INNER:2 * SSD_INNER]
                cd = jnp.broadcast_to(jnp.exp(tot), (SUBLANES, LANES))
                shared["cdx"] = _dot(pack_heads(split2_f32(cd)).astype(BF16), exp_s[:, 0:SSD_INNER])[0:1, :]
                shared["acs_t"] = acs.T
                shared["dt_t"] = dt.T

            def ssd_tasks(g):
                gs = slice(g * GROUP_WIDTH, (g + 1) * GROUP_WIDTH)
                b0 = SSD_INNER + g * SSD_STATE
                c0 = SSD_INNER + SSD_GROUPS * SSD_STATE + g * SSD_STATE
                grp = {"us": []}

                def head():
                    bm = xa_s[rows, b0:b0 + SSD_STATE].astype(BF16)
                    cm = xa_s[rows, c0:c0 + SSD_STATE].astype(BF16)
                    grp["cb"] = _dot_nt(cm, bm)
                    h_prev = state(hs_s[g])
                    grp["y_off"] = _dot(cm, h_prev.astype(BF16)) * shared["dfsx"][:, gs]
                    states = _dot_tn(bm, (xa_s[rows, gs] * shared["w1x"][:, gs]).astype(BF16))
                    hs_s[g] = h_prev * shared["cdx"][:, gs] + states

                def pair(pr):
                    def run():
                        acs, acs_t, dt_t = shared["acs"], shared["acs_t"], shared["dt_t"]
                        att = []
                        for j in range(2):
                            h = g * (SSD_HEADS // SSD_GROUPS) + 2 * pr + j
                            seg = jnp.exp(jnp.where(causal, acs[:, h:h + 1] - acs_t[h:h + 1, :], -jnp.inf))
                            att.append((grp["cb"] * seg * dt_t[h:h + 1, :]).astype(BF16))
                        ps = slice(pr * LANES, (pr + 1) * LANES)
                        gl = slice(g * GROUP_WIDTH + pr * LANES, g * GROUP_WIDTH + (pr + 1) * LANES)
                        xp = xa_s[rows, gl]
                        rhs = jnp.concatenate([jnp.where(first_half, xp, 0.0).astype(BF16),
                                               jnp.where(first_half, 0.0, xp).astype(BF16)], axis=0)
                        y = _dot(jnp.concatenate(att, axis=1), rhs)
                        y = y + grp["y_off"][:, ps] + dskip_ref[:, gl] * xp
                        grp["us"].append(y * _silu(pc[prow, Z0 + gl.start:Z0 + gl.stop]))
                    return run

                def norm():
                    us = grp["us"]
                    ssq = us[0] * us[0]
                    for u in us[1:]:
                        ssq = ssq + u * u
                    scale = lax.rsqrt(jnp.sum(ssq, axis=-1, keepdims=True) * (1.0 / GROUP_WIDTH) + EPS)
                    for pr, u in enumerate(us):
                        gl = slice(g * GROUP_WIDTH + pr * LANES, g * GROUP_WIDTH + (pr + 1) * LANES)
                        ymix_c[rows, gl] = (u * scale * normw_ref[:, gl]).astype(BF16)

                return [head] + [pair(pr) for pr in range(GROUP_WIDTH // LANES)] + [norm]

            def ret_scores(pr):
                def run():
                    ps = slice(pr * LANES, (pr + 1) * LANES)
                    qp = q_s[rows, ps]
                    kp = k_s[rows, ps]
                    kcat = jnp.concatenate([(kp * pair_masks[j]).astype(BF16) for j in range(2)], axis=0)
                    pair[pr] = {"s2": _dot_nt(qp.astype(BF16), kcat)}
                    vk = []
                    for j in range(2):
                        h = 2 * pr + j
                        vk.append((pc[prow, V0 + h * RET_V_DIM:V0 + (h + 1) * RET_V_DIM] * kd_s[h]).astype(BF16))
                    pair[pr]["st2"] = _dot_tn(kp.astype(BF16), jnp.concatenate(vk, axis=1))
                return run

            def ret_task(h):
                def run():
                    pr, j = divmod(h, 2)
                    ps = slice(pr * LANES, (pr + 1) * LANES)
                    js = slice(j * RET_V_DIM, (j + 1) * RET_V_DIM)
                    qp = q_s[rows, ps]
                    hl = slice(h * RET_V_DIM, (h + 1) * RET_V_DIM)
                    s = pair[pr]["s2"][:, js] * dec_s[h]
                    qd = (qp * qd_s[h]).astype(BF16)
                    vh = pc[prow, V0 + hl.start:V0 + hl.stop].astype(BF16)
                    r_prev = state(r_s[h])
                    y = _dot(jnp.concatenate([s.astype(BF16), qd], axis=1),
                             jnp.concatenate([vh, r_prev.astype(BF16)], axis=0))
                    r_s[h] = r_prev * float(np.exp(np.float32(CHUNK) * log_gamma[h])) + pair[pr]["st2"][:, js]
                    mu = jnp.mean(y, axis=-1, keepdims=True)
                    yc = y - mu
                    var = jnp.mean(yc * yc, axis=-1, keepdims=True)
                    yn = yc * lax.rsqrt(var + EPS) * gnw_ref[:, hl] + gnb_ref[:, hl]
                    gate = _silu(pc[prow, G0 + hl.start:G0 + hl.stop])
                    ymix_c[rows, SSD_INNER + hl.start:SSD_INNER + hl.stop] = (gate * yn).astype(BF16)
                return run

            pair = {}
            tasks = [pre]
            for g in range(SSD_GROUPS):
                tasks += ssd_tasks(g)
            for pr in range(RET_HEADS // 2):
                tasks += [ret_scores(pr), ret_task(2 * pr), ret_task(2 * pr + 1)]
            return tasks

        prep_tasks = [conv_task(j) for j in range(SSD_CONV_DIM // PROJ_BLOCK)] + [dt_task]
        prep_tasks += [rot_task(j) for j in range(RET_QK // LANES)]
        c_tasks = [(PREP_TASK_WEIGHT, task) for task in prep_tasks]
        for c in range(tile // CHUNK):
            c_tasks += [(1.0, task) for task in chunk_tasks(c)]

        m_tasks = p_tasks + o_tasks
        c_total = sum(w for w, _ in c_tasks)
        c_done = 0.0
        done = 0
        for w, task in c_tasks:
            task()
            c_done += w
            due = int(len(m_tasks) * c_done / c_total + 1e-9)
            for mt in m_tasks[done:due]:
                mt()
            done = due
        o_finish()

    last = ntiles - 1
    half(slice(0, tile), jnp.clip(2 * m - 1, 0, last), cos_a_ref, sin_a_ref,
         proj_a, proj_b, ymix_b, ymix_a, xa_a, dt_a, q_a, k_a)
    half(slice(tile, 2 * tile), jnp.clip(2 * m, 0, last), cos_b_ref, sin_b_ref,
         proj_b, proj_a, ymix_a, ymix_b, xa_b, dt_b, q_b, k_b)


def _mixer(x, cos_t, sin_t, w_in_p, conv_w, conv_b, dtb, alog, dskip, normw, gnw, gnb, w_out, ln_g, ln_b, tile):
    bsz, seqlen, _ = x.shape
    nt = seqlen // tile
    assert nt % 2 == 0
    ntiles = bsz * nt
    npairs = ntiles // 2
    const = lambda m: (0, 0)

    def cspec(shape):
        return pl.BlockSpec(shape, const, pipeline_mode=pl.Buffered(1))

    def pair_index(i):
        return (i // (nt // 2), i % (nt // 2), 0)

    def p_index(m):
        return pair_index(jnp.minimum(m, npairs - 1))

    def o_index(m):
        return pair_index(jnp.maximum(m - 1, 0))

    def rope_a_index(m):
        return (jnp.clip(2 * m - 1, 0, ntiles - 1) % nt, 0)

    def rope_b_index(m):
        return (jnp.clip(2 * m, 0, ntiles - 1) % nt, 0)

    per_half = [
        pltpu.VMEM((tile + SUBLANES, IN_PACKED), F32),
        pltpu.VMEM((tile, MIX_WIDTH), BF16),
        pltpu.VMEM((tile, SSD_CONV_DIM), F32),
        pltpu.VMEM((tile, LANES), F32),
        pltpu.VMEM((tile, RET_QK), F32),
        pltpu.VMEM((tile, RET_QK), F32),
    ]
    return pl.pallas_call(
        functools.partial(_mixer_kernel, tile=tile, nt=nt, ntiles=ntiles),
        grid=(npairs + 1,),
        in_specs=[
            pl.BlockSpec((1, 2 * tile, D_MODEL), p_index),
            pl.BlockSpec((1, 2 * tile, D_MODEL), o_index),
            pl.BlockSpec((tile, LANES), rope_a_index),
            pl.BlockSpec((tile, LANES), rope_a_index),
            pl.BlockSpec((tile, LANES), rope_b_index),
            pl.BlockSpec((tile, LANES), rope_b_index),
            cspec((D_MODEL, IN_PACKED)),
            cspec((SSD_CONV, SSD_CONV_DIM)),
            cspec((1, SSD_CONV_DIM)),
            cspec((1, LANES)),
            cspec((1, LANES)),
            cspec((1, SSD_INNER)),
            cspec((1, SSD_INNER)),
            cspec((1, RET_INNER)),
            cspec((1, RET_INNER)),
            cspec((MIX_WIDTH, D_MODEL)),
            cspec((1, D_MODEL)),
            cspec((1, D_MODEL)),
        ],
        out_specs=pl.BlockSpec((1, 2 * tile, D_MODEL), o_index),
        out_shape=jax.ShapeDtypeStruct((bsz, seqlen, D_MODEL), F32),
        scratch_shapes=[s for pair in zip(per_half, per_half) for s in pair] + [
            pltpu.VMEM((SUBLANES, SSD_CONV_DIM), F32),
            pltpu.VMEM((SSD_GROUPS, SSD_STATE, GROUP_WIDTH), F32),
            pltpu.VMEM((RET_HEADS, LANES, RET_V_DIM), F32),
            pltpu.VMEM((RET_HEADS, CHUNK, CHUNK), F32),
            pltpu.VMEM((RET_HEADS, CHUNK, LANES), F32),
            pltpu.VMEM((RET_HEADS, CHUNK, LANES), F32),
            pltpu.VMEM((LANES, 2 * SSD_INNER), BF16),
            pltpu.VMEM((CHUNK, CHUNK), BF16),
        ],
        compiler_params=pltpu.CompilerParams(
            dimension_semantics=("arbitrary",), vmem_limit_bytes=MIXER_VMEM_LIMIT_BYTES),
        name="mixer",
    )(x, x, cos_t, sin_t, cos_t, sin_t, w_in_p, conv_w, conv_b, dtb, alog, dskip, normw, gnw, gnb, w_out, ln_g, ln_b)


def _xattn_kernel(x_ref, kt_ref, v_ref, wq_ref, wo_ref, lng_ref, lnb_ref, out_ref, *scratch, tile):
    nstream = tile // XATTN_STREAM_ROWS
    q_s, o_s = scratch[:nstream], scratch[nstream:]

    def stream(i):
        rows = slice(i * XATTN_STREAM_ROWS, (i + 1) * XATTN_STREAM_ROWS)

        def q_task(h):
            def run():
                hs = slice(h * XATTN_HEAD_DIM, (h + 1) * XATTN_HEAD_DIM)
                q_s[i][:, hs] = _dot(x_ref[0, rows, :].astype(BF16), wq_ref[:, hs]).astype(BF16)
            return run

        def head_task(h):
            def run():
                hs = slice(h * XATTN_HEAD_DIM, (h + 1) * XATTN_HEAD_DIM)
                s = _dot(q_s[i][:, hs], kt_ref[0, hs, :]) * (XATTN_HEAD_DIM ** -0.5)
                m = jnp.max(s, axis=-1, keepdims=True)
                p = jnp.exp(s - m)
                p = p / jnp.sum(p, axis=-1, keepdims=True)
                o_s[i][:, hs] = _dot(p.astype(BF16), v_ref[0, :, hs]).astype(BF16)
            return run

        def out_task():
            xa = _dot(o_s[i][...], wo_ref[...])
            out_ref[0, rows, :] = _layer_norm(ALPHA * x_ref[0, rows, :] + xa, lng_ref[...], lnb_ref[...])

        tasks = [q_task(0)]
        for h in range(XATTN_HEADS):
            if h + 1 < XATTN_HEADS:
                tasks.append(q_task(h + 1))
            tasks.append(head_task(h))
        return tasks + [out_task]

    streams = [stream(i) for i in range(nstream)]
    for group in zip(*streams):
        for task in group:
            task()


def _xattn(x, kt, vm, wq, wo, ln_g, ln_b, tile):
    bsz, seqlen, _ = x.shape
    mlen = vm.shape[1]
    const = lambda b, t: (0, 0)

    def cspec(shape):
        return pl.BlockSpec(shape, const, pipeline_mode=pl.Buffered(1))

    assert tile % XATTN_STREAM_ROWS == 0
    return pl.pallas_call(
        functools.partial(_xattn_kernel, tile=tile),
        grid=(bsz, seqlen // tile),
        in_specs=[
            pl.BlockSpec((1, tile, D_MODEL), lambda b, t: (b, t, 0)),
            pl.BlockSpec((1, D_MODEL, mlen), lambda b, t: (b, 0, 0)),
            pl.BlockSpec((1, mlen, D_MODEL), lambda b, t: (b, 0, 0)),
            cspec((D_MODEL, D_MODEL)),
            cspec((D_MODEL, D_MODEL)),
            cspec((1, D_MODEL)),
            cspec((1, D_MODEL)),
        ],
        out_specs=pl.BlockSpec((1, tile, D_MODEL), lambda b, t: (b, t, 0)),
        out_shape=jax.ShapeDtypeStruct((bsz, seqlen, D_MODEL), F32),
        scratch_shapes=[pltpu.VMEM((XATTN_STREAM_ROWS, D_MODEL), BF16)
                        for _ in range(2 * (tile // XATTN_STREAM_ROWS))],
        compiler_params=pltpu.CompilerParams(
            dimension_semantics=("arbitrary", "arbitrary"), vmem_limit_bytes=VMEM_LIMIT_BYTES),
        name="xattn",
    )(x, kt, vm, wq, wo, ln_g, ln_b)


def _ffn_kernel(x_ref, wup_ref, bup_ref, convw_ref, convb_ref, wdown_ref, lng_ref, lnb_ref,
                out_ref, *scratch, tile):
    t = pl.program_id(1)
    sub = FFN_SUB_ROWS
    nsub = tile // sub
    nblk = D_FF_PAD // FFN_BLOCK
    acc = scratch[:nsub]
    ext = [scratch[nsub + s * 2 * nblk:nsub + (s + 1) * 2 * nblk] for s in range(nsub)]
    tail = slice(sub, sub + SUBLANES)

    @pl.when(t == 0)
    def _reset_halo():
        for e in ext[nsub - 1]:
            e[tail, :] = jnp.zeros((SUBLANES, FFN_BLOCK), F32)

    xb = [x_ref[0, s * sub:(s + 1) * sub, :].astype(BF16) for s in range(nsub)]

    def up(s, i):
        cs = slice(i * FFN_BLOCK, (i + 1) * FFN_BLOCK)
        ext[s][i][0:SUBLANES, :] = ext[(s - 1) % nsub][i][tail, :]
        ext[s][i][SUBLANES:SUBLANES + sub, :] = _dot(xb[s], wup_ref[:, cs]) + bup_ref[:, cs]

    def conv(s, i):
        cs = slice(i * FFN_BLOCK, (i + 1) * FFN_BLOCK)
        y = convb_ref[:, cs]
        for k in range(FFN_CONV):
            off = SUBLANES - (FFN_CONV - 1) + k
            y = y + convw_ref[k:k + 1, cs] * ext[s][i][off:off + sub, :]
        return y

    def down(s, j, gated):
        part = _dot(gated, wdown_ref[j * FFN_BLOCK:(j + 1) * FFN_BLOCK, :])
        if j == 0:
            acc[s][...] = part
        else:
            acc[s][...] += part
        if j == nblk - 1:
            rows = slice(s * sub, (s + 1) * sub)
            out_ref[0, rows, :] = _layer_norm(ALPHA * x_ref[0, rows, :] + acc[s][...], lng_ref[...], lnb_ref[...])

    items = [(s, j) for s in range(nsub) for j in range(nblk)]
    ahead = FFN_AHEAD

    def up_both(item):
        s, j = item
        up(s, j)
        up(s, nblk + j)

    for item in items[:ahead]:
        up_both(item)
    pending = None
    for idx, (s, j) in enumerate(items):
        if idx + ahead < len(items):
            up_both(items[idx + ahead])
        if pending is not None:
            down(*pending)
        pending = (s, j, (_silu(conv(s, j)) * conv(s, nblk + j)).astype(BF16))
    down(*pending)


def _ffn(x, w_up, b_up, conv_w, conv_b, w_down, ln_g, ln_b, tile):
    bsz, seqlen, _ = x.shape
    const = lambda b, t: (0, 0)

    def cspec(shape):
        return pl.BlockSpec(shape, const, pipeline_mode=pl.Buffered(1))

    return pl.pallas_call(
        functools.partial(_ffn_kernel, tile=tile),
        grid=(bsz, seqlen // tile),
        in_specs=[
            pl.BlockSpec((1, tile, D_MODEL), lambda b, t: (b, t, 0)),
            cspec((D_MODEL, 2 * D_FF_PAD)),
            cspec((1, 2 * D_FF_PAD)),
            cspec((FFN_CONV, 2 * D_FF_PAD)),
            cspec((1, 2 * D_FF_PAD)),
            cspec((D_FF_PAD, D_MODEL)),
            cspec((1, D_MODEL)),
            cspec((1, D_MODEL)),
        ],
        out_specs=pl.BlockSpec((1, tile, D_MODEL), lambda b, t: (b, t, 0)),
        out_shape=jax.ShapeDtypeStruct((bsz, seqlen, D_MODEL), F32),
        scratch_shapes=[pltpu.VMEM((FFN_SUB_ROWS, D_MODEL), F32) for _ in range(tile // FFN_SUB_ROWS)] + [
            pltpu.VMEM((FFN_SUB_ROWS + SUBLANES, FFN_BLOCK), F32)
            for _ in range((tile // FFN_SUB_ROWS) * 2 * D_FF_PAD // FFN_BLOCK)],
        compiler_params=pltpu.CompilerParams(
            dimension_semantics=("arbitrary", "arbitrary"), vmem_limit_bytes=VMEM_LIMIT_BYTES),
        name="ffn",
    )(x, w_up, b_up, conv_w, conv_b, w_down, ln_g, ln_b)


def _pair_layout(w):
    rows = w.shape[0]
    w = w.reshape(rows, RET_HEADS // 2, 2, 2, RET_QK_DIM // 2)
    return w.transpose(0, 1, 3, 2, 4).reshape(rows, RET_QK)


def _pad_last(a, n):
    return jnp.pad(a, [(0, 0)] * (a.ndim - 1) + [(0, n - a.shape[-1])])


def _pack_in_proj(w_in):
    o = np.cumsum((0, SSD_INNER, SSD_CONV_DIM, SSD_HEADS, RET_QK, RET_QK, RET_INNER, RET_INNER))
    z, xbc, dt, q, k, v, g = [w_in[:, o[i]:o[i + 1]] for i in range(7)]
    parts = [z, xbc, _pad_last(dt, LANES), _pair_layout(q), _pair_layout(k), v, g]
    return jnp.concatenate([p.astype(BF16) for p in parts], axis=-1)


def _pack_ffn_cols(a):
    return jnp.concatenate([_pad_last(a[..., :D_FF], D_FF_PAD), _pad_last(a[..., D_FF:], D_FF_PAD)], axis=-1)


def _rope_tables(seqlen):
    half = RET_QK_DIM // 2
    per_row = LANES // half
    freqs = 1.0 / (ROPE_BASE ** jnp.linspace(0.0, 1.0, half, dtype=F32))
    row = lax.broadcasted_iota(jnp.int32, (seqlen // per_row, LANES), 0)
    lane = lax.broadcasted_iota(jnp.int32, (seqlen // per_row, LANES), 1)
    pos = (row * per_row + lane // half).astype(F32)
    ang = pos * jnp.tile(freqs, per_row)[None, :]
    cos = jnp.cos(ang).reshape(seqlen, half)
    sin = jnp.sin(ang).reshape(seqlen, half)
    return jnp.concatenate([cos, cos, cos, cos], axis=-1), jnp.concatenate([-sin, -sin, sin, sin], axis=-1)


def _pick_tile(seqlen, want):
    tile = min(want, seqlen)
    assert seqlen % tile == 0 and tile % CHUNK == 0
    return tile


def kernel(x, mem, w_in, ssd_conv_w, ssd_conv_b, ssd_dt_bias, ssd_a_log, ssd_d, ssd_norm_w, ret_gn_w, ret_gn_b, w_mix_out, ln1_g, ln1_b, w_xq, w_xk, w_xv, w_xo, ln2_g, ln2_b, w_ffn_up, b_ffn_up, ffn_conv_w, ffn_conv_b, w_ffn_down, ln3_g, ln3_b):
    seqlen = x.shape[1]
    tile = _pick_tile(seqlen, MIXER_TILE)
    xattn_tile = _pick_tile(seqlen, XATTN_TILE)
    ffn_tile = _pick_tile(seqlen, FFN_TILE)
    cos_t, sin_t = _rope_tables(seqlen)
    row = lambda a: a.reshape(1, -1)
    for l in range(DEPTH):
        kt, vm = _memkv(mem, w_xk[l].astype(BF16), w_xv[l].astype(BF16))
        x = _mixer(
            x, cos_t, sin_t, _pack_in_proj(w_in[l]), ssd_conv_w[l], row(ssd_conv_b[l]),
            _pad_last(row(ssd_dt_bias[l]), LANES), _pad_last(row(ssd_a_log[l]), LANES),
            row(jnp.repeat(ssd_d[l], SSD_HEAD_DIM)), row(ssd_norm_w[l]), row(ret_gn_w[l]), row(ret_gn_b[l]),
            w_mix_out[l].astype(BF16), row(ln1_g[l]), row(ln1_b[l]), tile)
        x = _xattn(x, kt, vm, w_xq[l].astype(BF16), w_xo[l].astype(BF16), row(ln2_g[l]), row(ln2_b[l]),
                   xattn_tile)
        x = _ffn(
            x, _pack_ffn_cols(w_ffn_up[l].astype(BF16)), row(_pack_ffn_cols(b_ffn_up[l])),
            _pack_ffn_cols(ffn_conv_w[l]), row(_pack_ffn_cols(ffn_conv_b[l])),
            jnp.pad(w_ffn_down[l].astype(BF16), ((0, D_FF_PAD - D_FF), (0, 0))), row(ln3_g[l]), row(ln3_b[l]),
            ffn_tile)
    return x
```

```python
import functools

import numpy as np
import jax
import jax.numpy as jnp
from jax import lax
from jax.experimental import pallas as pl
from jax.experimental.pallas import tpu as pltpu

F32 = jnp.float32
BF16 = jnp.bfloat16

D_MODEL = 1024
DEPTH = 1
SSD_HEADS = 16
SSD_HEAD_DIM = 64
SSD_INNER = SSD_HEADS * SSD_HEAD_DIM
SSD_GROUPS = 2
SSD_STATE = 128
SSD_CONV = 4
SSD_CONV_DIM = SSD_INNER + 2 * SSD_GROUPS * SSD_STATE
GROUP_WIDTH = SSD_INNER // SSD_GROUPS
RET_HEADS = 8
RET_QK_DIM = 64
RET_V_DIM = 128
RET_QK = RET_HEADS * RET_QK_DIM
RET_INNER = RET_HEADS * RET_V_DIM
CHUNK = 128
ROPE_BASE = 10000.0
MIX_WIDTH = SSD_INNER + RET_INNER
XATTN_HEADS = 4
XATTN_HEAD_DIM = D_MODEL // XATTN_HEADS
D_FF = 2752
FFN_CONV = 3
ALPHA = (2.0 * DEPTH) ** 0.25
EPS = 1e-5

LANES = 128
SUBLANES = 8
VMEM_LIMIT_BYTES = 56 * 1024 * 1024
MIXER_VMEM_LIMIT_BYTES = 60 * 1024 * 1024

Z0 = 0
XBC0 = Z0 + SSD_INNER
DT0 = XBC0 + SSD_CONV_DIM
Q0 = DT0 + LANES
K0 = Q0 + RET_QK
V0 = K0 + RET_QK
G0 = V0 + RET_INNER
IN_PACKED = G0 + RET_INNER

D_FF_PAD = ((D_FF + LANES - 1) // LANES) * LANES
FFN_BLOCK = 256
PROJ_BLOCK = 256
OUT_BLOCK = 256
assert D_FF_PAD % FFN_BLOCK == 0 and DT0 % PROJ_BLOCK == 0 and (IN_PACKED - Q0) % PROJ_BLOCK == 0

MIXER_TILE = 256
XATTN_TILE = 2048
XATTN_STREAM_ROWS = 256
FFN_TILE = 256
FFN_AHEAD = 3
EXPAND_SLOTS = 4
PREP_TASK_WEIGHT = 0.2


def _dot(a, b):
    return jnp.dot(a, b, preferred_element_type=F32)


def _dot_nt(a, b):
    return lax.dot_general(a, b, (((1,), (1,)), ((), ())), preferred_element_type=F32)


def _dot_tn(a, b):
    return lax.dot_general(a, b, (((0,), (0,)), ((), ())), preferred_element_type=F32)


def _silu(v):
    return v / (1.0 + jnp.exp(-v))


def _layer_norm(r, g, b):
    mu = jnp.mean(r, axis=-1, keepdims=True)
    rc = r - mu
    var = jnp.mean(rc * rc, axis=-1, keepdims=True)
    return rc * lax.rsqrt(var + EPS) * g + b


def _split2(w):
    hi = w.astype(BF16)
    lo = (w - hi.astype(F32)).astype(BF16)
    return hi, lo


def _split3(w):
    hi = w.astype(BF16)
    r1 = w - hi.astype(F32)
    mid = r1.astype(BF16)
    lo = (r1 - mid.astype(F32)).astype(BF16)
    return hi, mid, lo


def _ret_log_gamma():
    return np.log1p(-np.exp2(-5.0 - np.arange(RET_HEADS, dtype=np.float32))).astype(np.float32)


def _pair_member(lane_idx):
    half = RET_QK_DIM // 2
    return jnp.bitwise_and(jnp.right_shift(lane_idx, half.bit_length() - 1), 1)


def _memkv_kernel(mem_ref, wk_ref, wv_ref, kt_ref, v_ref):
    m = mem_ref[0].astype(BF16)
    kt_ref[0] = lax.dot_general(wk_ref[...], m, (((0,), (1,)), ((), ())),
                                preferred_element_type=F32).astype(BF16)
    v_ref[0] = _dot(m, wv_ref[...]).astype(BF16)


def _memkv(mem, wk, wv):
    bsz, mlen, _ = mem.shape
    const = lambda b: (0, 0)
    return pl.pallas_call(
        _memkv_kernel,
        grid=(bsz,),
        in_specs=[
            pl.BlockSpec((1, mlen, D_MODEL), lambda b: (b, 0, 0)),
            pl.BlockSpec((D_MODEL, D_MODEL), const),
            pl.BlockSpec((D_MODEL, D_MODEL), const),
        ],
        out_specs=[
            pl.BlockSpec((1, D_MODEL, mlen), lambda b: (b, 0, 0)),
            pl.BlockSpec((1, mlen, D_MODEL), lambda b: (b, 0, 0)),
        ],
        out_shape=[
            jax.ShapeDtypeStruct((bsz, D_MODEL, mlen), BF16),
            jax.ShapeDtypeStruct((bsz, mlen, D_MODEL), BF16),
        ],
        compiler_params=pltpu.CompilerParams(
            dimension_semantics=("arbitrary",), vmem_limit_bytes=VMEM_LIMIT_BYTES),
        name="memkv",
    )(mem, wk, wv)


def _proj_blocks():
    blocks = [(c, PROJ_BLOCK) for c in range(0, DT0, PROJ_BLOCK)]
    blocks.append((DT0, LANES))
    blocks += [(c, PROJ_BLOCK) for c in range(Q0, IN_PACKED, PROJ_BLOCK)]
    return blocks


def _mixer_kernel(xp_ref, xr_ref, cos_a_ref, sin_a_ref, cos_b_ref, sin_b_ref,
                  win_ref, convw_ref, convb_ref, dtb_ref, alog_ref,
                  dskip_ref, normw_ref, gnw_ref, gnb_ref, wout_ref, lng_ref, lnb_ref,
                  out_ref,
                  proj_a, proj_b, ymix_a, ymix_b, xa_a, xa_b, dt_a, dt_b, q_a, q_b, k_a, k_b,
                  halo_s, hs_s, r_s, dec_s, qd_s, kd_s, exp_s, tri_s, *, tile, nt, ntiles):
    m = pl.program_id(0)
    log_gamma = _ret_log_gamma()
    top = SUBLANES

    @pl.when(m == 0)
    def _init():
        proj_b[...] = jnp.zeros_like(proj_b)
        ymix_a[...] = jnp.zeros_like(ymix_a)
        hs_s[...] = jnp.zeros_like(hs_s)
        r_s[...] = jnp.zeros_like(r_s)
        halo_s[...] = jnp.zeros_like(halo_s)
        row = lax.broadcasted_iota(jnp.int32, (CHUNK, CHUNK), 0)
        col = lax.broadcasted_iota(jnp.int32, (CHUNK, CHUNK), 1)
        tri_s[...] = jnp.where(row >= col, 1.0, 0.0).astype(BF16)
        erow = lax.broadcasted_iota(jnp.int32, (LANES, 2 * SSD_INNER), 0)
        ecol = lax.broadcasted_iota(jnp.int32, (LANES, 2 * SSD_INNER), 1)
        slot = erow // SSD_HEADS
        lo = (slot // 2) * SSD_INNER + (erow % SSD_HEADS) * SSD_HEAD_DIM
        hit = jnp.logical_and(slot < EXPAND_SLOTS, jnp.logical_and(ecol >= lo, ecol < lo + SSD_HEAD_DIM))
        exp_s[...] = jnp.where(hit, 1.0, 0.0).astype(BF16)
        dist = (row - col).astype(F32)
        pos = row.astype(F32)
        for h in range(RET_HEADS):
            lg = float(log_gamma[h])
            headmask = jnp.where(_pair_member(col) == (h % 2), 1.0, 0.0)
            dec_s[h] = (RET_QK_DIM ** -0.5) * jnp.exp(jnp.where(dist >= 0, dist * lg, -jnp.inf))
            qd_s[h] = jnp.exp((pos + 1.0) * lg) * headmask
            kd_s[h] = (RET_QK_DIM ** -0.5) * jnp.exp((CHUNK - 1.0 - pos) * lg)

    lane = lax.broadcasted_iota(jnp.int32, (1, LANES), 1)
    a_head = jnp.where(lane < SSD_HEADS, -jnp.exp(alog_ref[...]), 0.0)
    row = lax.broadcasted_iota(jnp.int32, (CHUNK, CHUNK), 0)
    col = lax.broadcasted_iota(jnp.int32, (CHUNK, CHUNK), 1)
    causal = row >= col
    first_half = col < SSD_HEAD_DIM
    pair_masks = [jnp.where(_pair_member(col) == j, 1.0, 0.0) for j in range(2)]

    head_lanes = lane < SSD_HEADS

    def pack_heads(parts):
        out = None
        for i, part in enumerate(parts):
            part = jnp.where(head_lanes, part, 0.0)
            if i:
                part = pltpu.roll(part, SSD_HEADS * i, 1)
            out = part if out is None else out + part
        return out

    def unpack_heads_sum(packed, nparts):
        out = packed
        for i in range(1, nparts):
            out = out + pltpu.roll(packed, LANES - SSD_HEADS * i, 1)
        return out

    def split2_f32(w):
        hi, lo = _split2(w)
        return [hi.astype(F32), lo.astype(F32)]

    def half(x_rows, c_tile, cos_ref, sin_ref, pp, pc, ymix_c, ymix_o, xa_s, dt_s, q_s, k_s):
        fresh = lax.rem(c_tile, nt) == 0

        xb = xp_ref[0, x_rows, :].astype(BF16)

        def p_task(c0, width):
            def run():
                pp[top:top + tile, c0:c0 + width] = _dot(xb, win_ref[:, c0:c0 + width])
            return run

        p_tasks = [p_task(c0, width) for c0, width in _proj_blocks()]

        def o_task(j):
            def run():
                cs = slice(j * OUT_BLOCK, (j + 1) * OUT_BLOCK)
                out_ref[0, x_rows, cs] = ALPHA * xr_ref[0, x_rows, cs] + _dot(ymix_o[...], wout_ref[:, cs])
            return run

        def o_finish():
            out_ref[0, x_rows, :] = _layer_norm(out_ref[0, x_rows, :], lng_ref[...], lnb_ref[...])

        o_tasks = [o_task(j) for j in range(D_MODEL // OUT_BLOCK)]

        def conv_task(j):
            def run():
                cs = slice(j * PROJ_BLOCK, (j + 1) * PROJ_BLOCK)
                ps = slice(XBC0 + j * PROJ_BLOCK, XBC0 + (j + 1) * PROJ_BLOCK)
                pc[0:SUBLANES, ps] = jnp.where(fresh, 0.0, halo_s[:, cs])
                conv = convb_ref[:, cs]
                for k in range(SSD_CONV):
                    off = top - (SSD_CONV - 1) + k
                    conv = conv + convw_ref[k:k + 1, cs] * pc[off:off + tile, ps]
                xa_s[:, cs] = _silu(conv)
                halo_s[:, cs] = pc[tile:tile + SUBLANES, ps]
            return run

        def dt_task():
            dtr = pc[top:top + tile, DT0:DT0 + LANES] + dtb_ref[...]
            dt_s[...] = jnp.maximum(dtr, 0.0) + jnp.log1p(jnp.exp(-jnp.abs(dtr)))

        def rot_task(j):
            def run():
                cos = cos_ref[...]
                sin = sin_ref[...]
                qj = pc[top:top + tile, Q0 + j * LANES:Q0 + (j + 1) * LANES]
                kj = pc[top:top + tile, K0 + j * LANES:K0 + (j + 1) * LANES]
                sl = slice(j * LANES, (j + 1) * LANES)
                q_s[:, sl] = qj * cos + pltpu.roll(qj, LANES // 2, 1) * sin
                k_s[:, sl] = kj * cos + pltpu.roll(kj, LANES // 2, 1) * sin
            return run

        def chunk_tasks(c):
            rows = slice(c * CHUNK, (c + 1) * CHUNK)
            prow = slice(top + c * CHUNK, top + (c + 1) * CHUNK)
            shared = {}

            def state(ref_value):
                return jnp.where(fresh, 0.0, ref_value) if c == 0 else ref_value

            def pre():
                dt = dt_s[rows, :]
                a = dt * a_head
                a_parts = pack_heads([p.astype(F32) for p in _split3(a)]).astype(BF16)
                acs = unpack_heads_sum(_dot(tri_s[...], a_parts), 3)
                tot = acs[CHUNK - 1:CHUNK, :]
                shared["acs"] = acs
                w1 = dt * jnp.exp(tot - acs)
                dfs = jnp.exp(acs)
                both = _dot(pack_heads(split2_f32(w1) + split2_f32(dfs)).astype(BF16), exp_s[...])
                shared["w1x"] = both[:, 0:SSD_INNER]
                shared["dfsx"] = both[:, SSD_INNER:2 * SSD_INNER]
                cd = jnp.broadcast_to(jnp.exp(tot), (SUBLANES, LANES))
                shared["cdx"] = _dot(pack_heads(split2_f32(cd)).astype(BF16), exp_s[:, 0:SSD_INNER])[0:1, :]
                shared["acs_t"] = acs.T
                shared["dt_t"] = dt.T

            def ssd_tasks(g):
                gs = slice(g * GROUP_WIDTH, (g + 1) * GROUP_WIDTH)
                b0 = SSD_INNER + g * SSD_STATE
                c0 = SSD_INNER + SSD_GROUPS * SSD_STATE + g * SSD_STATE
                grp = {"us": []}

                def head():
                    bm = xa_s[rows, b0:b0 + SSD_STATE].astype(BF16)
                    cm = xa_s[rows, c0:c0 + SSD_STATE].astype(BF16)
                    grp["cb"] = _dot_nt(cm, bm)
                    h_prev = state(hs_s[g])
                    grp["y_off"] = _dot(cm, h_prev.astype(BF16)) * shared["dfsx"][:, gs]
                    states = _dot_tn(bm, (xa_s[rows, gs] * shared["w1x"][:, gs]).astype(BF16))
                    hs_s[g] = h_prev * shared["cdx"][:, gs] + states

                def pair(pr):
                    def run():
                        acs, acs_t, dt_t = shared["acs"], shared["acs_t"], shared["dt_t"]
                        att = []
                        for j in range(2):
                            h = g * (SSD_HEADS // SSD_GROUPS) + 2 * pr + j
                            seg = jnp.exp(jnp.where(causal, acs[:, h:h + 1] - acs_t[h:h + 1, :], -jnp.inf))
                            att.append((grp["cb"] * seg * dt_t[h:h + 1, :]).astype(BF16))
                        ps = slice(pr * LANES, (pr + 1) * LANES)
                        gl = slice(g * GROUP_WIDTH + pr * LANES, g * GROUP_WIDTH + (pr + 1) * LANES)
                        xp = xa_s[rows, gl]
                        rhs = jnp.concatenate([jnp.where(first_half, xp, 0.0).astype(BF16),
                                               jnp.where(first_half, 0.0, xp).astype(BF16)], axis=0)
                        y = _dot(jnp.concatenate(att, axis=1), rhs)
                        y = y + grp["y_off"][:, ps] + dskip_ref[:, gl] * xp
                        grp["us"].append(y * _silu(pc[prow, Z0 + gl.start:Z0 + gl.stop]))
                    return run

                def norm():
                    us = grp["us"]
                    ssq = us[0] * us[0]
                    for u in us[1:]:
                        ssq = ssq + u * u
                    scale = lax.rsqrt(jnp.sum(ssq, axis=-1, keepdims=True) * (1.0 / GROUP_WIDTH) + EPS)
                    for pr, u in enumerate(us):
                        gl = slice(g * GROUP_WIDTH + pr * LANES, g * GROUP_WIDTH + (pr + 1) * LANES)
                        ymix_c[rows, gl] = (u * scale * normw_ref[:, gl]).astype(BF16)

                return [head] + [pair(pr) for pr in range(GROUP_WIDTH // LANES)] + [norm]

            def ret_scores(pr):
                def run():
                    ps = slice(pr * LANES, (pr + 1) * LANES)
                    qp = q_s[rows, ps]
                    kp = k_s[rows, ps]
                    kcat = jnp.concatenate([(kp * pair_masks[j]).astype(BF16) for j in range(2)], axis=0)
                    pair[pr] = {"s2": _dot_nt(qp.astype(BF16), kcat)}
                    vk = []
                    for j in range(2):
                        h = 2 * pr + j
                        vk.append((pc[prow, V0 + h * RET_V_DIM:V0 + (h + 1) * RET_V_DIM] * kd_s[h]).astype(BF16))
                    pair[pr]["st2"] = _dot_tn(kp.astype(BF16), jnp.concatenate(vk, axis=1))
                return run

            def ret_task(h):
                def run():
                    pr, j = divmod(h, 2)
                    ps = slice(pr * LANES, (pr + 1) * LANES)
                    js = slice(j * RET_V_DIM, (j + 1) * RET_V_DIM)
                    qp = q_s[rows, ps]
                    hl = slice(h * RET_V_DIM, (h + 1) * RET_V_DIM)
                    s = pair[pr]["s2"][:, js] * dec_s[h]
                    qd = (qp * qd_s[h]).astype(BF16)
                    vh = pc[prow, V0 + hl.start:V0 + hl.stop].astype(BF16)
                    r_prev = state(r_s[h])
                    y = _dot(jnp.concatenate([s.astype(BF16), qd], axis=1),
                             jnp.concatenate([vh, r_prev.astype(BF16)], axis=0))
                    r_s[h] = r_prev * float(np.exp(np.float32(CHUNK) * log_gamma[h])) + pair[pr]["st2"][:, js]
                    mu = jnp.mean(y, axis=-1, keepdims=True)
                    yc = y - mu
                    var = jnp.mean(yc * yc, axis=-1, keepdims=True)
                    yn = yc * lax.rsqrt(var + EPS) * gnw_ref[:, hl] + gnb_ref[:, hl]
                    gate = _silu(pc[prow, G0 + hl.start:G0 + hl.stop])
                    ymix_c[rows, SSD_INNER + hl.start:SSD_INNER + hl.stop] = (gate * yn).astype(BF16)
                return run

            pair = {}
            tasks = [pre]
            for g in range(SSD_GROUPS):
                tasks += ssd_tasks(g)
            for pr in range(RET_HEADS // 2):
                tasks += [ret_scores(pr), ret_task(2 * pr), ret_task(2 * pr + 1)]
            return tasks

        prep_tasks = [conv_task(j) for j in range(SSD_CONV_DIM // PROJ_BLOCK)] + [dt_task]
        prep_tasks += [rot_task(j) for j in range(RET_QK // LANES)]
        c_tasks = [(PREP_TASK_WEIGHT, task) for task in prep_tasks]
        for c in range(tile // CHUNK):
            c_tasks += [(1.0, task) for task in chunk_tasks(c)]

        m_tasks = p_tasks + o_tasks
        c_total = sum(w for w, _ in c_tasks)
        c_done = 0.0
        done = 0
        for w, task in c_tasks:
            task()
            c_done += w
            due = int(len(m_tasks) * c_done / c_total + 1e-9)
            for mt in m_tasks[done:due]:
                mt()
            done = due
        o_finish()

    last = ntiles - 1
    half(slice(0, tile), jnp.clip(2 * m - 1, 0, last), cos_a_ref, sin_a_ref,
         proj_a, proj_b, ymix_b, ymix_a, xa_a, dt_a, q_a, k_a)
    half(slice(tile, 2 * tile), jnp.clip(2 * m, 0, last), cos_b_ref, sin_b_ref,
         proj_b, proj_a, ymix_a, ymix_b, xa_b, dt_b, q_b, k_b)


def _mixer(x, cos_t, sin_t, w_in_p, conv_w, conv_b, dtb, alog, dskip, normw, gnw, gnb, w_out, ln_g, ln_b, tile):
    bsz, seqlen, _ = x.shape
    nt = seqlen // tile
    assert nt % 2 == 0
    ntiles = bsz * nt
    npairs = ntiles // 2
    const = lambda m: (0, 0)

    def cspec(shape):
        return pl.BlockSpec(shape, const, pipeline_mode=pl.Buffered(1))

    def pair_index(i):
        return (i // (nt // 2), i % (nt // 2), 0)

    def p_index(m):
        return pair_index(jnp.minimum(m, npairs - 1))

    def o_index(m):
        return pair_index(jnp.maximum(m - 1, 0))

    def rope_a_index(m):
        return (jnp.clip(2 * m - 1, 0, ntiles - 1) % nt, 0)

    def rope_b_index(m):
        return (jnp.clip(2 * m, 0, ntiles - 1) % nt, 0)

    per_half = [
        pltpu.VMEM((tile + SUBLANES, IN_PACKED), F32),
        pltpu.VMEM((tile, MIX_WIDTH), BF16),
        pltpu.VMEM((tile, SSD_CONV_DIM), F32),
        pltpu.VMEM((tile, LANES), F32),
        pltpu.VMEM((tile, RET_QK), F32),
        pltpu.VMEM((tile, RET_QK), F32),
    ]
    return pl.pallas_call(
        functools.partial(_mixer_kernel, tile=tile, nt=nt, ntiles=ntiles),
        grid=(npairs + 1,),
        in_specs=[
            pl.BlockSpec((1, 2 * tile, D_MODEL), p_index),
            pl.BlockSpec((1, 2 * tile, D_MODEL), o_index),
            pl.BlockSpec((tile, LANES), rope_a_index),
            pl.BlockSpec((tile, LANES), rope_a_index),
            pl.BlockSpec((tile, LANES), rope_b_index),
            pl.BlockSpec((tile, LANES), rope_b_index),
            cspec((D_MODEL, IN_PACKED)),
            cspec((SSD_CONV, SSD_CONV_DIM)),
            cspec((1, SSD_CONV_DIM)),
            cspec((1, LANES)),
            cspec((1, LANES)),
            cspec((1, SSD_INNER)),
            cspec((1, SSD_INNER)),
            cspec((1, RET_INNER)),
            cspec((1, RET_INNER)),
            cspec((MIX_WIDTH, D_MODEL)),
            cspec((1, D_MODEL)),
            cspec((1, D_MODEL)),
        ],
        out_specs=pl.BlockSpec((1, 2 * tile, D_MODEL), o_index),
        out_shape=jax.ShapeDtypeStruct((bsz, seqlen, D_MODEL), F32),
        scratch_shapes=[s for pair in zip(per_half, per_half) for s in pair] + [
            pltpu.VMEM((SUBLANES, SSD_CONV_DIM), F32),
            pltpu.VMEM((SSD_GROUPS, SSD_STATE, GROUP_WIDTH), F32),
            pltpu.VMEM((RET_HEADS, LANES, RET_V_DIM), F32),
            pltpu.VMEM((RET_HEADS, CHUNK, CHUNK), F32),
            pltpu.VMEM((RET_HEADS, CHUNK, LANES), F32),
            pltpu.VMEM((RET_HEADS, CHUNK, LANES), F32),
            pltpu.VMEM((LANES, 2 * SSD_INNER), BF16),
            pltpu.VMEM((CHUNK, CHUNK), BF16),
        ],
        compiler_params=pltpu.CompilerParams(
            dimension_semantics=("arbitrary",), vmem_limit_bytes=MIXER_VMEM_LIMIT_BYTES),
        name="mixer",
    )(x, x, cos_t, sin_t, cos_t, sin_t, w_in_p, conv_w, conv_b, dtb, alog, dskip, normw, gnw, gnb, w_out, ln_g, ln_b)


def _xattn_kernel(x_ref, kt_ref, v_ref, wq_ref, wo_ref, lng_ref, lnb_ref, out_ref, *scratch, tile):
    nstream = tile // XATTN_STREAM_ROWS
    q_s, o_s = scratch[:nstream], scratch[nstream:]

    def stream(i):
        rows = slice(i * XATTN_STREAM_ROWS, (i + 1) * XATTN_STREAM_ROWS)

        def q_task(h):
            def run():
                hs = slice(h * XATTN_HEAD_DIM, (h + 1) * XATTN_HEAD_DIM)
                q_s[i][:, hs] = _dot(x_ref[0, rows, :].astype(BF16), wq_ref[:, hs]).astype(BF16)
            return run

        def head_task(h):
            def run():
                hs = slice(h * XATTN_HEAD_DIM, (h + 1) * XATTN_HEAD_DIM)
                s = _dot(q_s[i][:, hs], kt_ref[0, hs, :]) * (XATTN_HEAD_DIM ** -0.5)
                m = jnp.max(s, axis=-1, keepdims=True)
                p = jnp.exp(s - m)
                p = p / jnp.sum(p, axis=-1, keepdims=True)
                o_s[i][:, hs] = _dot(p.astype(BF16), v_ref[0, :, hs]).astype(BF16)
            return run

        def out_task():
            xa = _dot(o_s[i][...], wo_ref[...])
            out_ref[0, rows, :] = _layer_norm(ALPHA * x_ref[0, rows, :] + xa, lng_ref[...], lnb_ref[...])

        tasks = [q_task(0)]
        for h in range(XATTN_HEADS):
            if h + 1 < XATTN_HEADS:
                tasks.append(q_task(h + 1))
            tasks.append(head_task(h))
        return tasks + [out_task]

    streams = [stream(i) for i in range(nstream)]
    for group in zip(*streams):
        for task in group:
            task()


def _xattn(x, kt, vm, wq, wo, ln_g, ln_b, tile):
    bsz, seqlen, _ = x.shape
    mlen = vm.shape[1]
    const = lambda b, t: (0, 0)

    def cspec(shape):
        return pl.BlockSpec(shape, const, pipeline_mode=pl.Buffered(1))

    assert tile % XATTN_STREAM_ROWS == 0
    return pl.pallas_call(
        functools.partial(_xattn_kernel, tile=tile),
        grid=(bsz, seqlen // tile),
        in_specs=[
            pl.BlockSpec((1, tile, D_MODEL), lambda b, t: (b, t, 0)),
            pl.BlockSpec((1, D_MODEL, mlen), lambda b, t: (b, 0, 0)),
            pl.BlockSpec((1, mlen, D_MODEL), lambda b, t: (b, 0, 0)),
            cspec((D_MODEL, D_MODEL)),
            cspec((D_MODEL, D_MODEL)),
            cspec((1, D_MODEL)),
            cspec((1, D_MODEL)),
        ],
        out_specs=pl.BlockSpec((1, tile, D_MODEL), lambda b, t: (b, t, 0)),
        out_shape=jax.ShapeDtypeStruct((bsz, seqlen, D_MODEL), F32),
        scratch_shapes=[pltpu.VMEM((XATTN_STREAM_ROWS, D_MODEL), BF16)
                        for _ in range(2 * (tile // XATTN_STREAM_ROWS))],
        compiler_params=pltpu.CompilerParams(
            dimension_semantics=("arbitrary", "arbitrary"), vmem_limit_bytes=VMEM_LIMIT_BYTES),
        name="xattn",
    )(x, kt, vm, wq, wo, ln_g, ln_b)


def _ffn_kernel(x_ref, wup_ref, bup_ref, convw_ref, convb_ref, wdown_ref, lng_ref, lnb_ref,
                out_ref, acc_s, *ext, tile):
    t = pl.program_id(1)
    tail = slice(tile, tile + SUBLANES)
    nblk = D_FF_PAD // FFN_BLOCK

    @pl.when(t == 0)
    def _reset_halo():
        for e in ext:
            e[tail, :] = jnp.zeros((SUBLANES, FFN_BLOCK), F32)

    x = x_ref[0]
    xb = x.astype(BF16)

    def up(i):
        cs = slice(i * FFN_BLOCK, (i + 1) * FFN_BLOCK)
        ext[i][0:SUBLANES, :] = ext[i][tail, :]
        ext[i][SUBLANES:SUBLANES + tile, :] = _dot(xb, wup_ref[:, cs]) + bup_ref[:, cs]

    def up_both(j):
        up(j)
        up(nblk + j)

    def conv(i):
        cs = slice(i * FFN_BLOCK, (i + 1) * FFN_BLOCK)
        y = convb_ref[:, cs]
        for k in range(FFN_CONV):
            off = SUBLANES - (FFN_CONV - 1) + k
            y = y + convw_ref[k:k + 1, cs] * ext[i][off:off + tile, :]
        return y

    def down(j, gated):
        part = _dot(gated, wdown_ref[j * FFN_BLOCK:(j + 1) * FFN_BLOCK, :])
        if j == 0:
            acc_s[...] = part
        else:
            acc_s[...] += part

    for j in range(FFN_AHEAD):
        up_both(j)
    pending = None
    for j in range(nblk):
        if j + FFN_AHEAD < nblk:
            up_both(j + FFN_AHEAD)
        if pending is not None:
            down(*pending)
        pending = (j, (_silu(conv(j)) * conv(nblk + j)).astype(BF16))
    down(*pending)
    out_ref[0] = _layer_norm(ALPHA * x + acc_s[...], lng_ref[...], lnb_ref[...])


def _ffn(x, w_up, b_up, conv_w, conv_b, w_down, ln_g, ln_b, tile):
    bsz, seqlen, _ = x.shape
    const = lambda b, t: (0, 0)

    def cspec(shape):
        return pl.BlockSpec(shape, const, pipeline_mode=pl.Buffered(1))

    return pl.pallas_call(
        functools.partial(_ffn_kernel, tile=tile),
        grid=(bsz, seqlen // tile),
        in_specs=[
            pl.BlockSpec((1, tile, D_MODEL), lambda b, t: (b, t, 0)),
            cspec((D_MODEL, 2 * D_FF_PAD)),
            cspec((1, 2 * D_FF_PAD)),
            cspec((FFN_CONV, 2 * D_FF_PAD)),
            cspec((1, 2 * D_FF_PAD)),
            cspec((D_FF_PAD, D_MODEL)),
            cspec((1, D_MODEL)),
            cspec((1, D_MODEL)),
        ],
        out_specs=pl.BlockSpec((1, tile, D_MODEL), lambda b, t: (b, t, 0)),
        out_shape=jax.ShapeDtypeStruct((bsz, seqlen, D_MODEL), F32),
        scratch_shapes=[pltpu.VMEM((tile, D_MODEL), F32)] + [
            pltpu.VMEM((tile + SUBLANES, FFN_BLOCK), F32) for _ in range(2 * D_FF_PAD // FFN_BLOCK)],
        compiler_params=pltpu.CompilerParams(
            dimension_semantics=("arbitrary", "arbitrary"), vmem_limit_bytes=VMEM_LIMIT_BYTES),
        name="ffn",
    )(x, w_up, b_up, conv_w, conv_b, w_down, ln_g, ln_b)


def _pair_layout(w):
    rows = w.shape[0]
    w = w.reshape(rows, RET_HEADS // 2, 2, 2, RET_QK_DIM // 2)
    return w.transpose(0, 1, 3, 2, 4).reshape(rows, RET_QK)


def _pad_last(a, n):
    return jnp.pad(a, [(0, 0)] * (a.ndim - 1) + [(0, n - a.shape[-1])])


def _pack_in_proj(w_in):
    o = np.cumsum((0, SSD_INNER, SSD_CONV_DIM, SSD_HEADS, RET_QK, RET_QK, RET_INNER, RET_INNER))
    z, xbc, dt, q, k, v, g = [w_in[:, o[i]:o[i + 1]] for i in range(7)]
    parts = [z, xbc, _pad_last(dt, LANES), _pair_layout(q), _pair_layout(k), v, g]
    return jnp.concatenate([p.astype(BF16) for p in parts], axis=-1)


def _pack_ffn_cols(a):
    return jnp.concatenate([_pad_last(a[..., :D_FF], D_FF_PAD), _pad_last(a[..., D_FF:], D_FF_PAD)], axis=-1)


def _rope_tables(seqlen):
    half = RET_QK_DIM // 2
    per_row = LANES // half
    freqs = 1.0 / (ROPE_BASE ** jnp.linspace(0.0, 1.0, half, dtype=F32))
    row = lax.broadcasted_iota(jnp.int32, (seqlen // per_row, LANES), 0)
    lane = lax.broadcasted_iota(jnp.int32, (seqlen // per_row, LANES), 1)
    pos = (row * per_row + lane // half).astype(F32)
    ang = pos * jnp.tile(freqs, per_row)[None, :]
    cos = jnp.cos(ang).reshape(seqlen, half)
    sin = jnp.sin(ang).reshape(seqlen, half)
    return jnp.concatenate([cos, cos, cos, cos], axis=-1), jnp.concatenate([-sin, -sin, sin, sin], axis=-1)


def _pick_tile(seqlen, want):
    tile = min(want, seqlen)
    assert seqlen % tile == 0 and tile % CHUNK == 0
    return tile


def kernel(x, mem, w_in, ssd_conv_w, ssd_conv_b, ssd_dt_bias, ssd_a_log, ssd_d, ssd_norm_w, ret_gn_w, ret_gn_b, w_mix_out, ln1_g, ln1_b, w_xq, w_xk, w_xv, w_xo, ln2_g, ln2_b, w_ffn_up, b_ffn_up, ffn_conv_w, ffn_conv_b, w_ffn_down, ln3_g, ln3_b):
    seqlen = x.shape[1]
    tile = _pick_tile(seqlen, MIXER_TILE)
    xattn_tile = _pick_tile(seqlen, XATTN_TILE)
    ffn_tile = _pick_tile(seqlen, FFN_TILE)
    cos_t, sin_t = _rope_tables(seqlen)
    row = lambda a: a.reshape(1, -1)
    for l in range(DEPTH):
        kt, vm = _memkv(mem, w_xk[l].astype(BF16), w_xv[l].astype(BF16))
        x = _mixer(
            x, cos_t, sin_t, _pack_in_proj(w_in[l]), ssd_conv_w[l], row(ssd_conv_b[l]),
            _pad_last(row(ssd_dt_bias[l]), LANES), _pad_last(row(ssd_a_log[l]), LANES),
            row(jnp.repeat(ssd_d[l], SSD_HEAD_DIM)), row(ssd_norm_w[l]), row(ret_gn_w[l]), row(ret_gn_b[l]),
            w_mix_out[l].astype(BF16), row(ln1_g[l]), row(ln1_b[l]), tile)
        x = _xattn(x, kt, vm, w_xq[l].astype(BF16), w_xo[l].astype(BF16), row(ln2_g[l]), row(ln2_b[l]),
                   xattn_tile)
        x = _ffn(
            x, _pack_ffn_cols(w_ffn_up[l].astype(BF16)), row(_pack_ffn_cols(b_ffn_up[l])),
            _pack_ffn_cols(ffn_conv_w[l]), row(_pack_ffn_cols(ffn_conv_b[l])),
            jnp.pad(w_ffn_down[l].astype(BF16), ((0, D_FF_PAD - D_FF), (0, 0))), row(ln3_g[l]), row(ln3_b[l]),
            ffn_tile)
    return x
```
